```python
import math
import jax, jax.numpy as jnp
from jax import lax
import numpy as np

D_MODEL = 1024
BATCH = 8
SEQ = 4096
DEPTH = 2

HEAD_DIM = D_MODEL // 16
GLA_HEADS = 4
GLA_DK = HEAD_DIM // 2
GLA_DV = HEAD_DIM
GLA_GATE_RANK = 16
GLA_GATE_TAU = 16.0
GLA_CHUNK = 64
CONV_CH = 4 * HEAD_DIM
CONV_GROUPS = 4
CONV_WIDTH = 31
DIFF_HEADS = 4
DIFF_D = HEAD_DIM
Q_BLOCK = 128
D_MIX = GLA_HEADS * GLA_DV + CONV_CH + DIFF_HEADS * 2 * DIFF_D
IN_SPLITS = (GLA_HEADS * GLA_DK, GLA_HEADS * GLA_DK, GLA_HEADS * GLA_DV, GLA_GATE_RANK,
             GLA_HEADS * GLA_DV, 2 * CONV_CH, DIFF_HEADS * 2 * DIFF_D,
             DIFF_HEADS * 2 * DIFF_D, DIFF_HEADS * 2 * DIFF_D)
N_IN = sum(IN_SPLITS)
N_GROUPS = 4
EXPERTS_PER_GROUP = 4
N_EXPERTS = N_GROUPS * EXPERTS_PER_GROUP
TOP_K = 2
D_EXPERT = D_MODEL // 4
EPS = 1e-6

kernel_name = 'hymba_style_hybrid_gla_conv_diffattn_hmoe'


def rms_norm(x, g):
    xf = x.astype(jnp.float32)
    y = xf * lax.rsqrt(jnp.mean(xf * xf, axis=-1, keepdims=True) + EPS)
    return (y * g.astype(jnp.float32)).astype(x.dtype)


def gla_mixer(q, k, v, gate_lr, out_gate, w_gate, b_gate, norm_g):
    B, S, _ = q.shape
    dt = q.dtype
    f32 = jnp.float32
    H, DK, DV, C = GLA_HEADS, GLA_DK, GLA_DV, GLA_CHUNK
    nc = S // C
    log_a = jax.nn.log_sigmoid((gate_lr @ w_gate + b_gate).astype(f32)) / GLA_GATE_TAU

    def to_chunks(t, d):
        return t.astype(f32).reshape(B, nc, C, H, d).transpose(1, 0, 3, 2, 4)

    qc = to_chunks(q, DK) * (DK ** -0.5)
    kc = to_chunks(k, DK)
    vc = to_chunks(v, DV)
    gc = to_chunks(log_a, DK)
    causal = jnp.tril(jnp.ones((C, C), dtype=bool))

    def step(state, inp):
        qi, ki, vi, gi = inp
        b = jnp.cumsum(gi, axis=2)
        o_inter = jnp.einsum('bhik,bhkv->bhiv', qi * jnp.exp(b), state)
        diff = b[:, :, :, None, :] - b[:, :, None, :, :]
        decay = jnp.exp(jnp.where(causal[:, :, None], diff, -jnp.inf))
        scores = jnp.einsum('bhik,bhjk,bhijk->bhij', qi, ki, decay)
        o = o_inter + jnp.einsum('bhij,bhjv->bhiv', scores, vi)
        b_last = b[:, :, -1:, :]
        new_state = (jnp.exp(b_last[:, :, 0, :])[..., None] * state
                     + jnp.einsum('bhjk,bhjv->bhkv', ki * jnp.exp(b_last - b), vi))
        return new_state, o

    s0 = jnp.zeros((B, H, DK, DV), f32)
    _, o = lax.scan(step, s0, (qc, kc, vc, gc))
    o = o.transpose(1, 0, 3, 2, 4).reshape(B, S, H, DV)
    o = rms_norm(o, norm_g).reshape(B, S, H * DV)
    o = o * jax.nn.silu(out_gate.astype(f32))
    return o.astype(dt)


def conv_mixer(u, w, b, gn_g, gn_b):
    dt = u.dtype
    a, gt = jnp.split(u, 2, axis=-1)
    z = a * jax.nn.sigmoid(gt)
    zp = jnp.pad(z, ((0, 0), (CONV_WIDTH - 1, 0), (0, 0)))
    y = lax.conv_general_dilated(zp, w[:, None, :].astype(dt), (1,), 'VALID',
                                 dimension_numbers=('NWC', 'WIO', 'NWC'),
                                 feature_group_count=CONV_CH) + b
    B, S, _ = y.shape
    yf = y.astype(jnp.float32).reshape(B, S, CONV_GROUPS, CONV_CH // CONV_GROUPS)
    mu = jnp.mean(yf, axis=-1, keepdims=True)
    var = jnp.mean(jnp.square(yf - mu), axis=-1, keepdims=True)
    yn = ((yf - mu) * lax.rsqrt(var + EPS)).reshape(B, S, CONV_CH)
    yn = yn * gn_g.astype(jnp.float32) + gn_b.astype(jnp.float32)
    return jax.nn.silu(yn).astype(dt)


def diff_attention(q, k, v, qn_g, kn_g, lq1, lk1, lq2, lk2, subln_g, lambda_init):
    B, S, _ = q.shape
    H, D = DIFF_HEADS, DIFF_D
    f32 = jnp.float32
    qh = rms_norm(q.reshape(B, S, H, 2, D), qn_g).transpose(0, 2, 3, 1, 4)
    kh = rms_norm(k.reshape(B, S, H, 2, D), kn_g).transpose(0, 2, 3, 1, 4)
    vh = v.reshape(B, S, H, 2 * D).transpose(0, 2, 1, 3)
    lam = (jnp.exp(jnp.sum(lq1.astype(f32) * lk1.astype(f32)))
           - jnp.exp(jnp.sum(lq2.astype(f32) * lk2.astype(f32))) + lambda_init)
    scale = D ** -0.5
    k_pos = jnp.arange(S)

    def block(i):
        start = i * Q_BLOCK
        qb = lax.dynamic_slice_in_dim(qh, start, Q_BLOCK, axis=3)
        s = jnp.einsum('bhmqd,bhmkd->bhmqk', qb, kh).astype(f32) * scale
        q_pos = start + jnp.arange(Q_BLOCK)
        s = jnp.where(k_pos[None, :] <= q_pos[:, None], s, -jnp.inf)
        p = jax.nn.softmax(s, axis=-1)
        a = p[:, :, 0] - lam * p[:, :, 1]
        return jnp.einsum('bhqk,bhkv->bhqv', a.astype(vh.dtype), vh)

    o = lax.map(block, jnp.arange(S // Q_BLOCK))
    o = o.transpose(1, 0, 3, 2, 4).reshape(B, S, H, 2 * D)
    o = rms_norm(o, subln_g) * (1.0 - lambda_init)
    return o.reshape(B, S, H * 2 * D)


def hier_moe(h, wg, bg, we, be, w_gate, w_up, w_down):
    B, S, D = h.shape
    f32 = jnp.float32
    t = h.reshape(-1, D)
    T = t.shape[0]
    glog = (t @ wg + bg).astype(f32)
    gprob = jax.nn.softmax(glog, axis=-1)
    g_onehot = jax.nn.one_hot(jnp.argmax(glog, axis=-1), N_GROUPS, dtype=f32)
    p_group = jnp.sum(gprob * g_onehot, axis=-1, keepdims=True)
    elog = (t @ we + be).astype(f32).reshape(T, N_GROUPS, EXPERTS_PER_GROUP)
    elog_sel = jnp.einsum('tge,tg->te', elog, g_onehot)
    eprob = jax.nn.softmax(elog_sel, axis=-1)
    top_p, top_i = lax.top_k(eprob, TOP_K)
    top_p = top_p / jnp.sum(top_p, axis=-1, keepdims=True)
    w_local = jnp.sum(jax.nn.one_hot(top_i, EXPERTS_PER_GROUP, dtype=f32) * top_p[..., None], axis=1)
    gates = (g_onehot[:, :, None] * w_local[:, None, :]).reshape(T, N_EXPERTS) * p_group
    gates = gates.astype(t.dtype)
    y = jnp.zeros_like(t)
    for e in range(N_EXPERTS):
        hid = jax.nn.silu(t @ w_gate[e]) * (t @ w_up[e])
        y = y + gates[:, e:e + 1] * (hid @ w_down[e])
    return y.reshape(B, S, D)


def setup_inputs(seed: int = 0) -> dict:
    key = jax.random.key(seed)
    ks = jax.random.split(key, 32)
    L, D, f32 = DEPTH, D_MODEL, jnp.float32

    def nrm(k, shape, scale):
        return jax.random.normal(k, shape, f32) * scale

    def gain(k, shape):
        return 1.0 + 0.02 * jax.random.normal(k, shape, f32)

    return {
        'x': nrm(ks[0], (BATCH, SEQ, D), 1.0),
        'mix_norm_g': gain(ks[1], (L, D)),
        'w_in': nrm(ks[2], (L, D, N_IN), D ** -0.5),
        'gla_gate_w': nrm(ks[3], (L, GLA_GATE_RANK, GLA_HEADS * GLA_DK), GLA_GATE_RANK ** -0.5),
        'gla_gate_b': nrm(ks[4], (L, GLA_HEADS * GLA_DK), 0.1),
        'gla_norm_g': gain(ks[5], (L, GLA_DV)),
        'conv_w': nrm(ks[6], (L, CONV_WIDTH, CONV_CH), CONV_WIDTH ** -0.5),
        'conv_b': nrm(ks[7], (L, CONV_CH), 0.02),
        'conv_norm_g': gain(ks[8], (L, CONV_CH)),
        'conv_norm_b': nrm(ks[9], (L, CONV_CH), 0.02),
        'diff_qnorm_g': gain(ks[10], (L, DIFF_D)),
        'diff_knorm_g': gain(ks[11], (L, DIFF_D)),
        'diff_lq1': nrm(ks[12], (L, DIFF_D), 0.1),
        'diff_lk1': nrm(ks[13], (L, DIFF_D), 0.1),
        'diff_lq2': nrm(ks[14], (L, DIFF_D), 0.1),
        'diff_lk2': nrm(ks[15], (L, DIFF_D), 0.1),
        'diff_subln_g': gain(ks[16], (L, 2 * DIFF_D)),
        'w_out': nrm(ks[17], (L, D_MIX, D), D_MIX ** -0.5),
        'ffn_norm_g': gain(ks[18], (L, D)),
        'router_group_w': nrm(ks[19], (L, D, N_GROUPS), D ** -0.5),
        'router_group_b': nrm(ks[20], (L, N_GROUPS), 0.01),
        'router_expert_w': nrm(ks[21], (L, D, N_EXPERTS), D ** -0.5),
        'router_expert_b': nrm(ks[22], (L, N_EXPERTS), 0.01),
        'expert_w_gate': nrm(ks[23], (L, N_EXPERTS, D, D_EXPERT), D ** -0.5),
        'expert_w_up': nrm(ks[24], (L, N_EXPERTS, D, D_EXPERT), D ** -0.5),
        'expert_w_down': nrm(ks[25], (L, N_EXPERTS, D_EXPERT, D), D_EXPERT ** -0.5),
    }


def reference(x, mix_norm_g, w_in, gla_gate_w, gla_gate_b, gla_norm_g, conv_w, conv_b,
              conv_norm_g, conv_norm_b, diff_qnorm_g, diff_knorm_g, diff_lq1, diff_lk1,
              diff_lq2, diff_lk2, diff_subln_g, w_out, ffn_norm_g, router_group_w,
              router_group_b, router_expert_w, router_expert_b, expert_w_gate, expert_w_up,
              expert_w_down):
    offsets = [int(o) for o in np.cumsum(IN_SPLITS)[:-1]]
    for l in range(DEPTH):
        hn = rms_norm(x, mix_norm_g[l])
        proj = hn @ w_in[l]
        gq, gk, gv, glr, gog, cu, dq, dk, dv = jnp.split(proj, offsets, axis=-1)
        o_gla = gla_mixer(gq, gk, gv, glr, gog, gla_gate_w[l], gla_gate_b[l], gla_norm_g[l])
        o_conv = conv_mixer(cu, conv_w[l], conv_b[l], conv_norm_g[l], conv_norm_b[l])
        lambda_init = 0.8 - 0.6 * math.exp(-0.3 * l)
        o_diff = diff_attention(dq, dk, dv, diff_qnorm_g[l], diff_knorm_g[l], diff_lq1[l],
                                diff_lk1[l], diff_lq2[l], diff_lk2[l], diff_subln_g[l], lambda_init)
        mix = jnp.concatenate([o_gla, o_conv, o_diff], axis=-1)
        x = x + mix @ w_out[l]
        x = x + hier_moe(rms_norm(x, ffn_norm_g[l]), router_group_w[l], router_group_b[l],
                         router_expert_w[l], router_expert_b[l], expert_w_gate[l],
                         expert_w_up[l], expert_w_down[l])
    return x
```

```python
import functools
import math

import numpy as np
import jax
import jax.numpy as jnp
from jax import lax
from jax.experimental import pallas as pl
from jax.experimental.pallas import tpu as pltpu

F32 = jnp.float32
BF16 = jnp.bfloat16
EPS = 1e-6
NEG_BIG = -1e30

LANES = 128
MIB = 1024 * 1024

GLA_HEADS = 4
GLA_DK = 32
GLA_DV = 64
GLA_RANK = 16
GLA_TAU = 16.0
GLA_CHUNK = 64
GLA_SUB = 16
CONV_CH = 256
CONV_GROUPS = 4
CONV_WIDTH = 31
CONV_HALO = 32
DIFF_HEADS = 4
DIFF_D = 64
N_GROUPS = 4
EPG = 4
N_EXPERTS = 16
D_EXPERT = 256

GLA_QK = GLA_HEADS * GLA_DK
GLA_V = GLA_HEADS * GLA_DV
GLA_IN = GLA_QK * 2 + GLA_V * 2 + LANES
DIFF_W = DIFF_HEADS * 2 * DIFF_D


def _dot(a, b):
    return jnp.dot(a, b, preferred_element_type=F32)


def _dot_nt(a, b):
    return lax.dot_general(a, b, (((1,), (1,)), ((), ())), preferred_element_type=F32)


def _dot_tn(a, b):
    return lax.dot_general(a, b, (((0,), (0,)), ((), ())), preferred_element_type=F32)


def _split_bf16(x):
    hi = x.astype(BF16)
    lo = (x - hi.astype(F32)).astype(BF16)
    return hi, lo


def _dot_hilo(x, m):
    hi, lo = _split_bf16(x)
    return _dot(hi, m) + _dot(lo, m)


def _sigmoid(x):
    return 1.0 / (1.0 + jnp.exp(-x))


def _group_mean_matrix(width, group):
    idx = np.arange(width) // group
    return jnp.asarray((idx[:, None] == idx[None, :]).astype(np.float32) / group, dtype=BF16)


def _params(vmem_mib, semantics):
    return pltpu.CompilerParams(dimension_semantics=semantics, vmem_limit_bytes=vmem_mib * MIB)


def _inproj_kernel(x_ref, g_ref, wa_ref, wc_ref, wq_ref, wk_ref, wvt_ref, qg_ref, kg_ref, gm_ref,
                   gla_ref, conv_ref, q_ref, k_ref, vt_ref):
    x = x_ref[...]
    ms = jnp.mean(x * x, axis=-1, keepdims=True)
    xn = (x * lax.rsqrt(ms + EPS) * g_ref[...]).astype(BF16)
    gla_ref[...] = _dot(xn, wa_ref[...]).astype(BF16)
    conv_ref[...] = _dot(xn, wc_ref[...]).astype(BF16)
    gm = gm_ref[...]

    def head_norm(t, g):
        msq = _dot((t * t).astype(BF16), gm)
        return t * lax.rsqrt(msq + EPS) * g

    q = head_norm(_dot(xn, wq_ref[...]), qg_ref[...])
    q_ref[...] = (q * (DIFF_D ** -0.5)).astype(BF16)
    k_ref[...] = head_norm(_dot(xn, wk_ref[...]), kg_ref[...]).astype(BF16)
    vt_ref[...] = _dot_nt(wvt_ref[...], xn).astype(BF16)


def _inproj(x2, g, wa, wc, wq, wk, wvt, qg, kg, *, tm):
    T, D = x2.shape
    gm = _group_mean_matrix(DIFF_W, DIFF_D)
    full = lambda shape: pl.BlockSpec(shape, lambda i: (0, 0))
    return pl.pallas_call(
        _inproj_kernel,
        grid=(T // tm,),
        in_specs=[
            pl.BlockSpec((tm, D), lambda i: (i, 0)),
            full((1, D)), full((D, GLA_IN)), full((D, 2 * CONV_CH)), full((D, DIFF_W)), full((D, DIFF_W)),
            full((DIFF_W, D)), full((1, DIFF_W)), full((1, DIFF_W)), full((DIFF_W, DIFF_W)),
        ],
        out_specs=[
            pl.BlockSpec((tm, GLA_IN), lambda i: (i, 0)),
            pl.BlockSpec((tm, 2 * CONV_CH), lambda i: (i, 0)),
            pl.BlockSpec((tm, DIFF_W), lambda i: (i, 0)),
            pl.BlockSpec((tm, DIFF_W), lambda i: (i, 0)),
            pl.BlockSpec((DIFF_W, tm), lambda i: (0, i)),
        ],
        out_shape=[
            jax.ShapeDtypeStruct((T, GLA_IN), BF16),
            jax.ShapeDtypeStruct((T, 2 * CONV_CH), BF16),
            jax.ShapeDtypeStruct((T, DIFF_W), BF16),
            jax.ShapeDtypeStruct((T, DIFF_W), BF16),
            jax.ShapeDtypeStruct((DIFF_W, T), BF16),
        ],
        compiler_params=_params(48, ("parallel",)),
        name="inproj",
    )(x2, g, wa, wc, wq, wk, wvt, qg, kg, gm)


def _gla_constants():
    C, SUB = GLA_CHUNK, GLA_SUB
    i = np.arange(C)[:, None]
    j = np.arange(C)[None, :]
    sub_start = (i // SUB) * SUB
    mats = [
        j <= i,
        j <= sub_start + SUB - 1,
        j < sub_start,
        j < sub_start - SUB,
        j < sub_start - 2 * SUB,
    ]
    cm = np.concatenate(mats, axis=0).astype(np.float32)
    col_j = np.tile(np.arange(C), GLA_HEADS)[None, :]
    dist = (i // SUB) - (col_j // SUB)
    sel = np.where(dist == 0, np.where(i >= col_j, 0, -1), np.where(dist > 0, dist, -1))
    kmask = (np.arange(GLA_HEADS * C)[:, None] // C) == (np.arange(GLA_QK)[None, :] // GLA_DK)
    vmask = (np.arange(GLA_HEADS * C)[:, None] // C) == (np.arange(GLA_V)[None, :] // GLA_DV)
    smask = (np.arange(GLA_V)[:, None] // GLA_DV) == (np.arange(GLA_QK)[None, :] // GLA_DK)
    return (jnp.asarray(cm, BF16), jnp.asarray(sel, jnp.int32), jnp.asarray(kmask, F32),
            jnp.asarray(vmask, F32), jnp.asarray(smask, F32))


def _gla_kernel(in_ref, wg_ref, bg_ref, ng_ref, cm_ref, sel_ref, kmask_ref, vmask_ref, smask_ref, gm_ref,
                o_ref, st_ref, *, nb):
    C = GLA_CHUNK

    @pl.when(pl.program_id(0) == 0)
    def _():
        st_ref[...] = jnp.zeros_like(st_ref)

    cm = cm_ref[...]
    sel = sel_ref[...]
    kmask = kmask_ref[...] > 0.5
    vmask = vmask_ref[...] > 0.5
    smask = smask_ref[...]
    gm = gm_ref[...]
    scale = GLA_DK ** -0.5
    for bi in range(nb):
        blk = in_ref[bi]
        q = blk[:, 0:GLA_QK].astype(F32)
        k = blk[:, GLA_QK:2 * GLA_QK].astype(F32)
        v = blk[:, 2 * GLA_QK:2 * GLA_QK + GLA_V]
        og = blk[:, 2 * GLA_QK + GLA_V:2 * GLA_QK + 2 * GLA_V].astype(F32)
        glr = blk[:, 2 * GLA_QK + 2 * GLA_V:]
        z = _dot(glr, wg_ref[...]) + bg_ref[...]
        la = (jnp.minimum(z, 0.0) - jnp.log(1.0 + jnp.exp(-jnp.abs(z)))) * (1.0 / GLA_TAU)
        la_hi, la_lo = _split_bf16(la)
        cs = _dot(cm, jnp.concatenate([la_hi, la_lo], axis=1))
        cs = cs[:, :GLA_QK] + cs[:, GLA_QK:]
        b, pend, pprev, r1, r2 = (cs[n * C:(n + 1) * C] for n in range(5))
        blast = pend[C - 1:C]
        qs = q * scale
        qe = qs * jnp.exp(b - pprev)
        lhs = jnp.concatenate([qe, qe * jnp.exp(pprev - r1), qe * jnp.exp(pprev - r2)], axis=0).astype(BF16)
        ke = (k * jnp.exp(pend - b)).astype(BF16)
        kdg = (k * jnp.exp(pprev - b)).astype(BF16)
        zero = jnp.zeros((), BF16)
        kbd = jnp.where(kmask, jnp.concatenate([ke] * GLA_HEADS, axis=0), zero)
        kbd_dg = jnp.where(kmask, jnp.concatenate([kdg] * GLA_HEADS, axis=0), zero)
        s_off = _dot_nt(lhs, kbd)
        s_dg = _dot_nt(qe.astype(BF16), kbd_dg)
        scores = jnp.where(sel == 0, s_dg,
                           jnp.where(sel == 1, s_off[0:C],
                                     jnp.where(sel == 2, s_off[C:2 * C],
                                               jnp.where(sel == 3, s_off[2 * C:3 * C], 0.0))))
        vbd = jnp.where(vmask, jnp.concatenate([v] * GLA_HEADS, axis=0), zero)
        st = st_ref[bi]
        o = _dot(scores.astype(BF16), vbd) + _dot_nt((qs * jnp.exp(b)).astype(BF16), st.astype(BF16))
        kd = (k * jnp.exp(blast - b)).astype(BF16)
        st_ref[bi] = st * jnp.exp(blast) + _dot_tn(v, kd) * smask
        ms = _dot_hilo(o * o, gm)
        on = o * lax.rsqrt(ms + EPS) * ng_ref[...]
        o_ref[bi] = (on * (og * _sigmoid(og))).astype(o_ref.dtype)


def _gla(gla_in, wg, bg, ng, *, B, S):
    C = GLA_CHUNK
    x3 = gla_in.reshape(B, S, GLA_IN)
    cm, sel, kmask, vmask, smask = _gla_constants()
    gm = _group_mean_matrix(GLA_V, GLA_DV)
    consts = (wg, bg, ng, cm, sel, kmask, vmask, smask, gm)
    full = lambda a: pl.BlockSpec(a.shape, lambda c: (0,) * a.ndim)
    out = pl.pallas_call(
        functools.partial(_gla_kernel, nb=B),
        grid=(S // C,),
        in_specs=[pl.BlockSpec((B, C, GLA_IN), lambda c: (0, c, 0))] + [full(a) for a in consts],
        out_specs=pl.BlockSpec((B, C, GLA_V), lambda c: (0, c, 0)),
        out_shape=jax.ShapeDtypeStruct((B, S, GLA_V), BF16),
        scratch_shapes=[pltpu.VMEM((B, GLA_V, GLA_QK), F32)],
        compiler_params=_params(32, ("arbitrary",)),
        name="gla",
    )(x3, *consts)
    return out.reshape(B * S, GLA_V)


def _conv_kernel(halo_ref, cur_ref, w_ref, b_ref, g_ref, beta_ref, gm_ref, o_ref, z_sc, *, ts, sub):
    def glu(u):
        u = u.astype(F32)
        return u[:, :CONV_CH] * _sigmoid(u[:, CONV_CH:])

    first = pl.program_id(1) == 0
    zh = glu(halo_ref[...])
    z_sc[0:CONV_HALO, :] = jnp.where(first, 0.0, zh)
    z_sc[CONV_HALO:, :] = glu(cur_ref[...])
    gm = gm_ref[...]
    off = CONV_HALO - (CONV_WIDTH - 1)
    for r0 in range(0, ts, sub):
        acc = jnp.broadcast_to(b_ref[...], (sub, CONV_CH))
        for w in range(CONV_WIDTH):
            acc = acc + z_sc[r0 + off + w:r0 + off + w + sub, :] * w_ref[w:w + 1, :]
        mu = _dot_hilo(acc, gm)
        d = acc - mu
        var = _dot_hilo(d * d, gm)
        yn = d * lax.rsqrt(var + EPS) * g_ref[...] + beta_ref[...]
        o_ref[r0:r0 + sub, :] = (yn * _sigmoid(yn)).astype(o_ref.dtype)


def _conv(conv_in, w, b, g, beta, *, B, S, ts):
    T = B * S
    nt = S // ts
    hb = ts // CONV_HALO
    gm = _group_mean_matrix(CONV_CH, CONV_CH // CONV_GROUPS)
    full = lambda a: pl.BlockSpec(a.shape, lambda bb, i: (0,) * a.ndim)
    consts = (w, b, g, beta, gm)
    return pl.pallas_call(
        functools.partial(_conv_kernel, ts=ts, sub=64),
        grid=(B, nt),
        in_specs=[
            pl.BlockSpec((CONV_HALO, 2 * CONV_CH), lambda bb, i: (jnp.maximum((bb * nt + i) * hb - 1, 0), 0)),
            pl.BlockSpec((ts, 2 * CONV_CH), lambda bb, i: (bb * nt + i, 0)),
        ] + [full(a) for a in consts],
        out_specs=pl.BlockSpec((ts, CONV_CH), lambda bb, i: (bb * nt + i, 0)),
        out_shape=jax.ShapeDtypeStruct((T, CONV_CH), BF16),
        scratch_shapes=[pltpu.VMEM((ts + CONV_HALO, CONV_CH), F32)],
        compiler_params=_params(32, ("parallel", "parallel")),
        name="conv",
    )(conv_in, conv_in, *consts)


def _attn_kernel(q_ref, k_ref, vt_ref, lq1_ref, lk1_ref, lq2_ref, lk2_ref, sg_ref, o_ref, acc_sc,
                 *, tq, lambda_init):
    tk = tq
    i = pl.program_id(2)
    q = q_ref[...]
    lane = lax.broadcasted_iota(jnp.int32, (tk, 2 * DIFF_D), 1)
    first_map = lane < DIFF_D
    zero = jnp.zeros((), BF16)
    acc_sc[...] = jnp.zeros_like(acc_sc)

    def step(j, carry, masked):
        m1, l1, m2, l2 = carry
        start = pl.multiple_of(j * tk, tk)
        kb = k_ref[pl.ds(start, tk), :]
        kk = jnp.concatenate([jnp.where(first_map, kb, zero), jnp.where(first_map, zero, kb)], axis=0)
        s = _dot_nt(kk, q)
        vt = vt_ref[:, pl.ds(start, tk)]
        if masked:
            key = lax.broadcasted_iota(jnp.int32, (tk, tq), 0)
            qry = lax.broadcasted_iota(jnp.int32, (tk, tq), 1)
            keep = key <= qry
        out = []
        for m, (mp, lp) in enumerate(((m1, l1), (m2, l2))):
            sm = s[m * tk:(m + 1) * tk]
            if masked:
                sm = jnp.where(keep, sm, NEG_BIG)
            mn = jnp.maximum(mp, jnp.max(sm, axis=0, keepdims=True))
            alpha = jnp.exp(mp - mn)
            p = jnp.exp(sm - mn)
            ln = alpha * lp + jnp.sum(p, axis=0, keepdims=True)
            acc_sc[m] = alpha * acc_sc[m] + _dot(vt, p.astype(BF16))
            out += [mn, ln]
        return tuple(out)

    init = (jnp.full((1, tq), NEG_BIG, F32), jnp.zeros((1, tq), F32)) * 2
    carry = lax.fori_loop(0, i, lambda j, c: step(j, c, False), init)
    m1, l1, m2, l2 = step(i, carry, True)

    lam = (jnp.exp(jnp.sum(lq1_ref[...] * lk1_ref[...], keepdims=True))
           - jnp.exp(jnp.sum(lq2_ref[...] * lk2_ref[...], keepdims=True)) + lambda_init)
    a = acc_sc[0] / l1 - lam * (acc_sc[1] / l2)
    ms = jnp.mean(a * a, axis=0, keepdims=True)
    y = a * lax.rsqrt(ms + EPS) * sg_ref[...] * (1.0 - lambda_init)
    o_ref[...] = y.T.astype(o_ref.dtype)


def _attn(qn, kn, vt, lq1, lk1, lq2, lk2, sg, *, B, S, tq, lambda_init):
    T = B * S
    nq = S // tq
    W = 2 * DIFF_D
    small = lambda a: pl.BlockSpec(a.shape, lambda b, h, i: (0,) * a.ndim)
    consts = (lq1, lk1, lq2, lk2, sg)
    return pl.pallas_call(
        functools.partial(_attn_kernel, tq=tq, lambda_init=lambda_init),
        grid=(B, DIFF_HEADS, nq),
        in_specs=[
            pl.BlockSpec((tq, W), lambda b, h, i: (b * nq + i, h)),
            pl.BlockSpec((S, W), lambda b, h, i: (b, h)),
            pl.BlockSpec((W, S), lambda b, h, i: (h, b)),
        ] + [small(a) for a in consts],
        out_specs=pl.BlockSpec((tq, W), lambda b, h, i: (b * nq + i, h)),
        out_shape=jax.ShapeDtypeStruct((T, DIFF_W), BF16),
        scratch_shapes=[pltpu.VMEM((2, W, tq), F32)],
        compiler_params=_params(48, ("parallel", "parallel", "arbitrary")),
        name="diffattn",
    )(qn, kn, vt, *consts)


def _route(logits):
    lane = lax.broadcasted_iota(jnp.int32, logits.shape, 1)
    big = jnp.int32(1 << 20)
    is_g = lane < N_GROUPS
    gl = jnp.where(is_g, logits, NEG_BIG)
    gmax = jnp.max(gl, axis=-1, keepdims=True)
    gsum = jnp.sum(jnp.where(is_g, jnp.exp(gl - gmax), 0.0), axis=-1, keepdims=True)
    gidx = jnp.min(jnp.where(is_g & (gl == gmax), lane, big), axis=-1, keepdims=True)
    p_group = 1.0 / gsum
    e_lane = lane - N_GROUPS
    sel = (e_lane >= 0) & (e_lane < N_EXPERTS) & ((e_lane // EPG) == gidx)
    el = jnp.where(sel, logits, NEG_BIG)
    emax = jnp.max(el, axis=-1, keepdims=True)
    eexp = jnp.where(sel, jnp.exp(el - emax), 0.0)
    eprob = eexp / jnp.sum(eexp, axis=-1, keepdims=True)
    cand = jnp.where(sel, eprob, -1.0)
    top1 = jnp.max(cand, axis=-1, keepdims=True)
    idx1 = jnp.min(jnp.where(sel & (cand == top1), lane, big), axis=-1, keepdims=True)
    cand2 = jnp.where(lane == idx1, -1.0, cand)
    top2 = jnp.max(cand2, axis=-1, keepdims=True)
    idx2 = jnp.min(jnp.where(sel & (cand2 == top2), lane, big), axis=-1, keepdims=True)
    denom = top1 + top2
    w = jnp.where(lane == idx1, top1 / denom, jnp.where(lane == idx2, top2 / denom, 0.0))
    return w * p_group


def _outproj_kernel(x_ref, og_ref, oc_ref, od_ref, wg_ref, wc_ref, wd_ref, fg_ref, rw_ref, rb_ref,
                    h_ref, hn_ref, gates_ref):
    h = (x_ref[...] + _dot(og_ref[...], wg_ref[...]) + _dot(oc_ref[...], wc_ref[...])
         + _dot(od_ref[...], wd_ref[...]))
    h_ref[...] = h
    ms = jnp.mean(h * h, axis=-1, keepdims=True)
    hn = h * lax.rsqrt(ms + EPS) * fg_ref[...]
    hn_ref[...] = hn.astype(BF16)
    hi, lo = _split_bf16(hn)
    whi = rw_ref[0]
    wlo = rw_ref[1]
    logits = _dot(hi, whi) + _dot(lo, whi) + _dot(hi, wlo) + rb_ref[...]
    gates_ref[...] = _route(logits)


def _outproj(x2, o_gla, o_conv, o_diff, wg, wc, wd, fg, rw, rb, *, tm):
    T, D = x2.shape
    tile = lambda w: pl.BlockSpec((tm, w), lambda i: (i, 0))
    full = lambda a: pl.BlockSpec(a.shape, lambda i: (0,) * a.ndim)
    consts = (wg, wc, wd, fg, rw, rb)
    return pl.pallas_call(
        _outproj_kernel,
        grid=(T // tm,),
        in_specs=[tile(D), tile(GLA_V), tile(CONV_CH), tile(DIFF_W)] + [full(a) for a in consts],
        out_specs=[tile(D), tile(D), tile(LANES)],
        out_shape=[
            jax.ShapeDtypeStruct((T, D), F32),
            jax.ShapeDtypeStruct((T, D), BF16),
            jax.ShapeDtypeStruct((T, LANES), F32),
        ],
        compiler_params=_params(48, ("parallel",)),
        name="outproj",
    )(x2, o_gla, o_conv, o_diff, *consts)


def _moe_kernel(h_ref, hn_ref, gates_ref, wgu_ref, wd_ref, o_ref, hid_sc):
    t = hn_ref[...]
    gates = gates_ref[...]
    lane = lax.broadcasted_iota(jnp.int32, gates.shape, 1)
    for e in range(N_EXPERTS):
        gu = _dot(t, wgu_ref[e])
        g = gu[:, :D_EXPERT]
        u = gu[:, D_EXPERT:]
        ge = jnp.sum(jnp.where(lane == N_GROUPS + e, gates, 0.0), axis=-1, keepdims=True)
        hid_sc[:, e * D_EXPERT:(e + 1) * D_EXPERT] = (g * _sigmoid(g) * u * ge).astype(BF16)
    o_ref[...] = h_ref[...] + _dot(hid_sc[...], wd_ref[...])


def _moe(h, hn, gates, wgu, wd, *, tm):
    T, D = h.shape
    tile = lambda w: pl.BlockSpec((tm, w), lambda i: (i, 0))
    return pl.pallas_call(
        _moe_kernel,
        grid=(T // tm,),
        in_specs=[tile(D), tile(D), tile(LANES),
                  pl.BlockSpec(wgu.shape, lambda i: (0, 0, 0), pipeline_mode=pl.Buffered(1)),
                  pl.BlockSpec(wd.shape, lambda i: (0, 0), pipeline_mode=pl.Buffered(1))],
        out_specs=tile(D),
        out_shape=jax.ShapeDtypeStruct((T, D), F32),
        scratch_shapes=[pltpu.VMEM((tm, N_EXPERTS * D_EXPERT), BF16)],
        compiler_params=_params(56, ("parallel",)),
        name="moe",
    )(h, hn, gates, wgu, wd)


def _layer(x2, l, p, *, B, S):
    D = x2.shape[1]
    row = lambda a: a.reshape(1, -1).astype(F32)
    w_in = p["w_in"][l]
    o = np.cumsum([0, GLA_QK, GLA_QK, GLA_V, GLA_RANK, GLA_V, 2 * CONV_CH, DIFF_W, DIFF_W, DIFF_W])
    seg = lambda n: w_in[:, o[n]:o[n + 1]]
    wa = jnp.concatenate([seg(0), seg(1), seg(2), seg(4), seg(3),
                          jnp.zeros((D, LANES - GLA_RANK), F32)], axis=1).astype(BF16)
    wc = seg(5).astype(BF16)
    wq = seg(6).astype(BF16)
    wk = seg(7).astype(BF16)
    wvt = seg(8).T.astype(BF16)
    qg = row(jnp.tile(p["diff_qnorm_g"][l], 2 * DIFF_HEADS))
    kg = row(jnp.tile(p["diff_knorm_g"][l], 2 * DIFF_HEADS))
    gla_in, conv_in, qn, kn, vt = _inproj(x2, row(p["mix_norm_g"][l]), wa, wc, wq, wk, wvt, qg, kg, tm=512)

    wgate = jnp.zeros((LANES, GLA_QK), F32).at[:GLA_RANK].set(p["gla_gate_w"][l]).astype(BF16)
    o_gla = _gla(gla_in, wgate, row(p["gla_gate_b"][l]), row(jnp.tile(p["gla_norm_g"][l], GLA_HEADS)), B=B, S=S)

    o_conv = _conv(conv_in, p["conv_w"][l].astype(F32), row(p["conv_b"][l]), row(p["conv_norm_g"][l]),
                   row(p["conv_norm_b"][l]), B=B, S=S, ts=min(512, S))

    lambda_init = 0.8 - 0.6 * math.exp(-0.3 * l)
    o_diff = _attn(qn, kn, vt, row(p["diff_lq1"][l]), row(p["diff_lk1"][l]), row(p["diff_lq2"][l]),
                   row(p["diff_lk2"][l]), p["diff_subln_g"][l].reshape(-1, 1).astype(F32),
                   B=B, S=S, tq=min(512, S), lambda_init=lambda_init)

    w_out = p["w_out"][l].astype(BF16)
    rw = jnp.zeros((D, LANES), F32)
    rw = rw.at[:, :N_GROUPS].set(p["router_group_w"][l]).at[:, N_GROUPS:N_GROUPS + N_EXPERTS].set(p["router_expert_w"][l])
    rw_hi, rw_lo = _split_bf16(rw)
    rb = jnp.zeros((1, LANES), F32)
    rb = rb.at[0, :N_GROUPS].set(p["router_group_b"][l]).at[0, N_GROUPS:N_GROUPS + N_EXPERTS].set(p["router_expert_b"][l])
    h, hn, gates = _outproj(x2, o_gla, o_conv, o_diff, w_out[:GLA_V], w_out[GLA_V:GLA_V + CONV_CH],
                            w_out[GLA_V + CONV_CH:], row(p["ffn_norm_g"][l]), jnp.stack([rw_hi, rw_lo]), rb, tm=512)

    wgu = jnp.concatenate([p["expert_w_gate"][l], p["expert_w_up"][l]], axis=-1).astype(BF16)
    wd = p["expert_w_down"][l].reshape(N_EXPERTS * D_EXPERT, D).astype(BF16)
    return _moe(h, hn, gates, wgu, wd, tm=512)


def kernel(x, mix_norm_g, w_in, gla_gate_w, gla_gate_b, gla_norm_g, conv_w, conv_b, conv_norm_g, conv_norm_b, diff_qnorm_g, diff_knorm_g, diff_lq1, diff_lk1, diff_lq2, diff_lk2, diff_subln_g, w_out, ffn_norm_g, router_group_w, router_group_b, router_expert_w, router_expert_b, expert_w_gate, expert_w_up, expert_w_down):
    p = dict(mix_norm_g=mix_norm_g, w_in=w_in, gla_gate_w=gla_gate_w, gla_gate_b=gla_gate_b,
             gla_norm_g=gla_norm_g, conv_w=conv_w, conv_b=conv_b, conv_norm_g=conv_norm_g,
             conv_norm_b=conv_norm_b, diff_qnorm_g=diff_qnorm_g, diff_knorm_g=diff_knorm_g,
             diff_lq1=diff_lq1, diff_lk1=diff_lk1, diff_lq2=diff_lq2, diff_lk2=diff_lk2,
             diff_subln_g=diff_subln_g, w_out=w_out, ffn_norm_g=ffn_norm_g,
             router_group_w=router_group_w, router_group_b=router_group_b,
             router_expert_w=router_expert_w, router_expert_b=router_expert_b,
             expert_w_gate=expert_w_gate, expert_w_up=expert_w_up, expert_w_down=expert_w_down)
    B, S, D = x.shape
    assert D == 16 * DIFF_D and w_in.shape[-1] == 2 * GLA_QK + 2 * GLA_V + GLA_RANK + 2 * CONV_CH + 3 * DIFF_W
    x2 = x.reshape(B * S, D).astype(F32)
    for l in range(w_in.shape[0]):
        x2 = _layer(x2, l, p, B=B, S=S)
    return x2.reshape(B, S, D)
```

```python
import functools
import math

import numpy as np
import jax
import jax.numpy as jnp
from jax import lax
from jax.experimental import pallas as pl
from jax.experimental.pallas import tpu as pltpu

F32 = jnp.float32
BF16 = jnp.bfloat16
EPS = 1e-6
NEG_BIG = -1e30
LOG2E = math.log2(math.e)

LANES = 128
SUBLANES = 8
MIB = 1024 * 1024

GLA_HEADS = 4
GLA_DK = 32
GLA_DV = 64
GLA_RANK = 16
GLA_TAU = 16.0
GLA_CHUNK = 64
GLA_SUB = 16
CONV_CH = 256
CONV_GROUPS = 4
CONV_WIDTH = 31
CONV_HALO = 32
DIFF_HEADS = 4
DIFF_D = 64
N_GROUPS = 4
EPG = 4
N_EXPERTS = 16
D_EXPERT = 256

GLA_QK = GLA_HEADS * GLA_DK
GLA_V = GLA_HEADS * GLA_DV
GLA_IN = GLA_QK * 2 + GLA_V * 2 + LANES
DIFF_W = DIFF_HEADS * 2 * DIFF_D


def _dot(a, b):
    return jnp.dot(a, b, preferred_element_type=F32)


def _dot_nt(a, b):
    return lax.dot_general(a, b, (((1,), (1,)), ((), ())), preferred_element_type=F32)


def _dot_tn(a, b):
    return lax.dot_general(a, b, (((0,), (0,)), ((), ())), preferred_element_type=F32)


def _split_bf16(x):
    hi = x.astype(BF16)
    lo = (x - hi.astype(F32)).astype(BF16)
    return hi, lo


def _dot_hilo(x, m):
    hi, lo = _split_bf16(x)
    return _dot(hi, m) + _dot(lo, m)


def _sigmoid(x):
    return 1.0 / (1.0 + jnp.exp(-x))


def _group_mean_matrix(width, group):
    idx = np.arange(width) // group
    return jnp.asarray((idx[:, None] == idx[None, :]).astype(np.float32) / group, dtype=BF16)


def _params(vmem_mib, semantics):
    return pltpu.CompilerParams(dimension_semantics=semantics, vmem_limit_bytes=vmem_mib * MIB)


def _inproj_kernel(x_ref, g_ref, wa_ref, wc_ref, wq_ref, wk_ref, wvt_ref, qg_ref, kg_ref, gm_ref,
                   gla_ref, conv_ref, q_ref, k_ref, vt_ref):
    x = x_ref[...]
    ms = jnp.mean(x * x, axis=-1, keepdims=True)
    xn = (x * lax.rsqrt(ms + EPS) * g_ref[...]).astype(BF16)
    gla_ref[...] = _dot(xn, wa_ref[...]).astype(BF16)
    conv_ref[...] = _dot(xn, wc_ref[...]).astype(BF16)
    gm = gm_ref[...]

    def head_norm(t, g):
        msq = _dot((t * t).astype(BF16), gm)
        return t * lax.rsqrt(msq + EPS) * g

    q = head_norm(_dot(xn, wq_ref[...]), qg_ref[...])
    q_ref[...] = (q * (DIFF_D ** -0.5 * LOG2E)).astype(BF16)
    k_ref[...] = head_norm(_dot(xn, wk_ref[...]), kg_ref[...]).astype(BF16)
    vt_ref[...] = _dot_nt(wvt_ref[...], xn).astype(BF16)


def _inproj(x2, g, wa, wc, wq, wk, wvt, qg, kg, *, tm):
    T, D = x2.shape
    gm = _group_mean_matrix(DIFF_W, DIFF_D)
    full = lambda shape: pl.BlockSpec(shape, lambda i: (0, 0))
    return pl.pallas_call(
        _inproj_kernel,
        grid=(T // tm,),
        in_specs=[
            pl.BlockSpec((tm, D), lambda i: (i, 0)),
            full((1, D)), full((D, GLA_IN)), full((D, 2 * CONV_CH)), full((D, DIFF_W)), full((D, DIFF_W)),
            full((DIFF_W, D)), full((1, DIFF_W)), full((1, DIFF_W)), full((DIFF_W, DIFF_W)),
        ],
        out_specs=[
            pl.BlockSpec((tm, GLA_IN), lambda i: (i, 0)),
            pl.BlockSpec((tm, 2 * CONV_CH), lambda i: (i, 0)),
            pl.BlockSpec((tm, DIFF_W), lambda i: (i, 0)),
            pl.BlockSpec((tm, DIFF_W), lambda i: (i, 0)),
            pl.BlockSpec((DIFF_W, tm), lambda i: (0, i)),
        ],
        out_shape=[
            jax.ShapeDtypeStruct((T, GLA_IN), BF16),
            jax.ShapeDtypeStruct((T, 2 * CONV_CH), BF16),
            jax.ShapeDtypeStruct((T, DIFF_W), BF16),
            jax.ShapeDtypeStruct((T, DIFF_W), BF16),
            jax.ShapeDtypeStruct((DIFF_W, T), BF16),
        ],
        compiler_params=_params(48, ("parallel",)),
        name="inproj",
    )(x2, g, wa, wc, wq, wk, wvt, qg, kg, gm)


def _gla_constants(nb):
    C, SUB = GLA_CHUNK, GLA_SUB
    r = np.arange(nb * C)
    tril = ((r[:, None] // C) == (r[None, :] // C)) & (r[None, :] <= r[:, None])
    i = np.arange(C)[:, None]
    col_j = np.tile(np.arange(C), GLA_HEADS)[None, :]
    dist = (i // SUB) - (col_j // SUB)
    sel = np.where(dist == 0, np.where(i >= col_j, 0, -1), np.where(dist > 0, dist, -1))
    kmask = (np.arange(GLA_HEADS * C)[:, None] // C) == (np.arange(GLA_QK)[None, :] // GLA_DK)
    vmask = (np.arange(GLA_HEADS * C)[:, None] // C) == (np.arange(GLA_V)[None, :] // GLA_DV)
    smask = (np.arange(GLA_V)[:, None] // GLA_DV) == (np.arange(GLA_QK)[None, :] // GLA_DK)
    return (jnp.asarray(tril, BF16), jnp.asarray(sel, jnp.int32), jnp.asarray(kmask, F32),
            jnp.asarray(vmask, F32), jnp.asarray(smask, F32))


def _gla_kernel(in_ref, wg_ref, bg_ref, ng_ref, tril_ref, sel_ref, kmask_ref, vmask_ref, smask_ref, gm_ref,
                o_ref, st_ref, *, nb):
    C, SUB = GLA_CHUNK, GLA_SUB
    R = nb * C
    nsub = C // SUB

    @pl.when(pl.program_id(0) == 0)
    def _():
        st_ref[...] = jnp.zeros_like(st_ref)

    sel = sel_ref[...]
    kmask = kmask_ref[...] > 0.5
    vmask = vmask_ref[...] > 0.5
    smask = smask_ref[...]
    scale = GLA_DK ** -0.5
    zero = jnp.zeros((), BF16)

    x = in_ref[...].reshape(R, GLA_IN)
    q = x[:, 0:GLA_QK].astype(F32)
    k = x[:, GLA_QK:2 * GLA_QK].astype(F32)
    v = x[:, 2 * GLA_QK:2 * GLA_QK + GLA_V]
    og = x[:, 2 * GLA_QK + GLA_V:2 * GLA_QK + 2 * GLA_V].astype(F32)
    glr = x[:, 2 * GLA_QK + 2 * GLA_V:]
    z = _dot(glr, wg_ref[...]) + bg_ref[...]
    la = (jnp.minimum(z, 0.0) - jnp.log(1.0 + jnp.exp(-jnp.abs(z)))) * (1.0 / GLA_TAU)
    la_hi, la_lo = _split_bf16(la)
    b2 = _dot(tril_ref[...], jnp.concatenate([la_hi, la_lo], axis=1))
    b = b2[:, :GLA_QK] + b2[:, GLA_QK:]
    ends = b.reshape(R // SUB, SUB, GLA_QK)[:, SUB - 1:SUB, :]
    sub_idx = lax.broadcasted_iota(jnp.int32, ends.shape, 0) & (nsub - 1)

    def back(n):
        shifted = jnp.concatenate([jnp.zeros((n, 1, GLA_QK), F32), ends[:-n]], axis=0)
        return jnp.where(sub_idx >= n, shifted, 0.0)

    expand = lambda t: jnp.broadcast_to(t, (R // SUB, SUB, GLA_QK)).reshape(R, GLA_QK)
    pprev_s, r1_s, r2_s = back(1), back(2), back(3)
    pend = expand(ends)
    pprev = expand(pprev_s)
    qs = q * scale
    qe = qs * jnp.exp(b - pprev)
    qe1 = (qe * expand(jnp.exp(pprev_s - r1_s))).astype(BF16)
    qe2 = (qe * expand(jnp.exp(pprev_s - r2_s))).astype(BF16)
    qe = qe.astype(BF16)
    qeb = (qs * jnp.exp(b)).astype(BF16)
    ke = (k * jnp.exp(pend - b)).astype(BF16)
    kdg = (k * jnp.exp(pprev - b)).astype(BF16)

    outs = []
    for bi in range(nb):
        sl = slice(bi * C, (bi + 1) * C)
        blast = ends[bi * nsub + nsub - 1]
        kbd = jnp.where(kmask, jnp.concatenate([ke[sl]] * GLA_HEADS, axis=0), zero)
        kbd_dg = jnp.where(kmask, jnp.concatenate([kdg[sl]] * GLA_HEADS, axis=0), zero)
        s_off = _dot_nt(jnp.concatenate([qe[sl], qe1[sl], qe2[sl]], axis=0), kbd)
        s_dg = _dot_nt(qe[sl], kbd_dg)
        scores = jnp.where(sel == 0, s_dg,
                           jnp.where(sel == 1, s_off[0:C],
                                     jnp.where(sel == 2, s_off[C:2 * C],
                                               jnp.where(sel == 3, s_off[2 * C:3 * C], 0.0))))
        vbd = jnp.where(vmask, jnp.concatenate([v[sl]] * GLA_HEADS, axis=0), zero)
        st = st_ref[bi]
        outs.append(_dot(scores.astype(BF16), vbd) + _dot_nt(qeb[sl], st.astype(BF16)))
        kd = (k[sl] * jnp.exp(blast - b[sl])).astype(BF16)
        st_ref[bi] = st * jnp.exp(blast) + _dot_tn(v[sl], kd) * smask

    o = jnp.concatenate(outs, axis=0)
    hi, lo = _split_bf16(o * o)
    ms2 = _dot(jnp.concatenate([hi, lo], axis=0), gm_ref[...])
    on = o * lax.rsqrt(ms2[:R] + ms2[R:] + EPS) * ng_ref[...]
    o_ref[...] = (on * (og * _sigmoid(og))).reshape(nb, C, GLA_V).astype(o_ref.dtype)


def _gla(gla_in, wg, bg, ng, *, B, S):
    C = GLA_CHUNK
    x3 = gla_in.reshape(B, S, GLA_IN)
    cm, sel, kmask, vmask, smask = _gla_constants(B)
    gm = _group_mean_matrix(GLA_V, GLA_DV)
    consts = (wg, bg, ng, cm, sel, kmask, vmask, smask, gm)
    full = lambda a: pl.BlockSpec(a.shape, lambda c: (0,) * a.ndim)
    out = pl.pallas_call(
        functools.partial(_gla_kernel, nb=B),
        grid=(S // C,),
        in_specs=[pl.BlockSpec((B, C, GLA_IN), lambda c: (0, c, 0))] + [full(a) for a in consts],
        out_specs=pl.BlockSpec((B, C, GLA_V), lambda c: (0, c, 0)),
        out_shape=jax.ShapeDtypeStruct((B, S, GLA_V), BF16),
        scratch_shapes=[pltpu.VMEM((B, GLA_V, GLA_QK), F32)],
        compiler_params=_params(32, ("arbitrary",)),
        name="gla",
    )(x3, *consts)
    return out.reshape(B * S, GLA_V)


def _conv_kernel(halo_ref, cur_ref, w_ref, b_ref, g_ref, beta_ref, gm_ref, o_ref, z_sc, *, ts, sub):
    def glu(u):
        u = u.astype(F32)
        return u[:, :CONV_CH] * _sigmoid(u[:, CONV_CH:])

    first = pl.program_id(1) == 0
    zh = glu(halo_ref[...])
    z_sc[0, 0:CONV_HALO, :] = jnp.where(first, 0.0, zh)
    z_sc[0, CONV_HALO:, :] = glu(cur_ref[...])
    span = ts + CONV_HALO - SUBLANES
    for s in range(1, SUBLANES):
        z_sc[s, 0:span, :] = z_sc[0, s:s + span, :]
    gm = gm_ref[...]
    off = CONV_HALO - (CONV_WIDTH - 1)
    for r0 in range(0, ts, sub):
        acc = jnp.broadcast_to(b_ref[...], (sub, CONV_CH))
        for w in range(CONV_WIDTH):
            shift = (off + w) % SUBLANES
            base = r0 + off + w - shift
            acc = acc + z_sc[shift, base:base + sub, :] * w_ref[w:w + 1, :]
        mu = _dot_hilo(acc, gm)
        d = acc - mu
        var = _dot_hilo(d * d, gm)
        yn = d * lax.rsqrt(var + EPS) * g_ref[...] + beta_ref[...]
        o_ref[r0:r0 + sub, :] = (yn * _sigmoid(yn)).astype(o_ref.dtype)


def _conv(conv_in, w, b, g, beta, *, B, S, ts):
    T = B * S
    nt = S // ts
    hb = ts // CONV_HALO
    gm = _group_mean_matrix(CONV_CH, CONV_CH // CONV_GROUPS)
    full = lambda a: pl.BlockSpec(a.shape, lambda bb, i: (0,) * a.ndim)
    consts = (w, b, g, beta, gm)
    return pl.pallas_call(
        functools.partial(_conv_kernel, ts=ts, sub=64),
        grid=(B, nt),
        in_specs=[
            pl.BlockSpec((CONV_HALO, 2 * CONV_CH), lambda bb, i: (jnp.maximum((bb * nt + i) * hb - 1, 0), 0)),
            pl.BlockSpec((ts, 2 * CONV_CH), lambda bb, i: (bb * nt + i, 0)),
        ] + [full(a) for a in consts],
        out_specs=pl.BlockSpec((ts, CONV_CH), lambda bb, i: (bb * nt + i, 0)),
        out_shape=jax.ShapeDtypeStruct((T, CONV_CH), BF16),
        scratch_shapes=[pltpu.VMEM((SUBLANES, ts + CONV_HALO, CONV_CH), F32)],
        compiler_params=_params(32, ("parallel", "parallel")),
        name="conv",
    )(conv_in, conv_in, *consts)


def _attn_kernel(q_ref, k_ref, vt_ref, lq1_ref, lk1_ref, lq2_ref, lk2_ref, sg_ref, o_ref, acc_sc, sa_sc, sb_sc,
                 *, tq, lambda_init):
    tk = tq
    i = pl.program_id(2)
    q = q_ref[...]
    lane = lax.broadcasted_iota(jnp.int32, (tk, 2 * DIFF_D), 1)
    first_map = lane < DIFF_D
    zero = jnp.zeros((), BF16)
    acc_sc[...] = jnp.zeros_like(acc_sc)

    def scores_into(j, dst):
        start = pl.multiple_of(j * tk, tk)
        kb = k_ref[pl.ds(start, tk), :]
        kk = jnp.concatenate([jnp.where(first_map, kb, zero), jnp.where(first_map, zero, kb)], axis=0)
        dst[...] = _dot_nt(kk, q)

    def softmax_pv(j, src, carry, masked):
        start = pl.multiple_of(j * tk, tk)
        vt = vt_ref[:, pl.ds(start, tk)]
        if masked:
            key = lax.broadcasted_iota(jnp.int32, (tk, tq), 0)
            qry = lax.broadcasted_iota(jnp.int32, (tk, tq), 1)
            keep = key <= qry
        out = []
        for m in range(2):
            mp, lp = carry[2 * m], carry[2 * m + 1]
            sm = src[m * tk:(m + 1) * tk, :]
            if masked:
                sm = jnp.where(keep, sm, NEG_BIG)
            mn = jnp.maximum(mp, jnp.max(sm, axis=0, keepdims=True))
            alpha = jnp.exp2(mp - mn)
            p = jnp.exp2(sm - mn)
            ln = alpha * lp + jnp.sum(p, axis=0, keepdims=True)
            acc_sc[m] = alpha * acc_sc[m] + _dot(vt, p.astype(BF16))
            out += [mn, ln]
        return tuple(out)

    def pair(jj, carry):
        j = 2 * jj
        scores_into(j + 1, sb_sc)
        carry = softmax_pv(j, sa_sc, carry, False)
        scores_into(j + 2, sa_sc)
        return softmax_pv(j + 1, sb_sc, carry, False)

    def tail_odd(carry):
        scores_into(i, sb_sc)
        carry = softmax_pv(i - 1, sa_sc, carry, False)
        return softmax_pv(i, sb_sc, carry, True)

    def tail_even(carry):
        return softmax_pv(i, sa_sc, carry, True)

    scores_into(0, sa_sc)
    init = (jnp.full((1, tq), NEG_BIG, F32), jnp.zeros((1, tq), F32)) * 2
    carry = lax.fori_loop(0, lax.shift_right_logical(i, 1), pair, init)
    m1, l1, m2, l2 = lax.cond(lax.rem(i, 2) == 1, tail_odd, tail_even, carry)

    lam = (jnp.exp(jnp.sum(lq1_ref[...] * lk1_ref[...], keepdims=True))
           - jnp.exp(jnp.sum(lq2_ref[...] * lk2_ref[...], keepdims=True)) + lambda_init)
    a = acc_sc[0] / l1 - lam * (acc_sc[1] / l2)
    ms = jnp.mean(a * a, axis=0, keepdims=True)
    y = a * lax.rsqrt(ms + EPS) * sg_ref[...] * (1.0 - lambda_init)
    o_ref[...] = y.T.astype(o_ref.dtype)


def _attn(qn, kn, vt, lq1, lk1, lq2, lk2, sg, *, B, S, tq, lambda_init):
    T = B * S
    nq = S // tq
    W = 2 * DIFF_D
    small = lambda a: pl.BlockSpec(a.shape, lambda b, h, i: (0,) * a.ndim)
    consts = (lq1, lk1, lq2, lk2, sg)
    return pl.pallas_call(
        functools.partial(_attn_kernel, tq=tq, lambda_init=lambda_init),
        grid=(B, DIFF_HEADS, nq),
        in_specs=[
            pl.BlockSpec((tq, W), lambda b, h, i: (b * nq + i, h)),
            pl.BlockSpec((S, W), lambda b, h, i: (b, h)),
            pl.BlockSpec((W, S), lambda b, h, i: (h, b)),
        ] + [small(a) for a in consts],
        out_specs=pl.BlockSpec((tq, W), lambda b, h, i: (b * nq + i, h)),
        out_shape=jax.ShapeDtypeStruct((T, DIFF_W), BF16),
        scratch_shapes=[pltpu.VMEM((2, W, tq), F32), pltpu.VMEM((2 * tq, tq), F32), pltpu.VMEM((2 * tq, tq), F32)],
        compiler_params=_params(48, ("parallel", "parallel", "arbitrary")),
        name="diffattn",
    )(qn, kn, vt, *consts)


def _route(logits):
    lt = logits.T
    row = lambda r: lt[r:r + 1, :]
    g = [row(i) for i in range(N_GROUPS)]
    gmax = functools.reduce(jnp.maximum, g)
    p_group = 1.0 / sum(jnp.exp(gi - gmax) for gi in g)
    taken = jnp.zeros(gmax.shape, jnp.bool_)
    is_g = []
    for gi in g:
        hit = jnp.logical_and(gi == gmax, jnp.logical_not(taken))
        is_g.append(hit)
        taken = jnp.logical_or(taken, hit)
    el = []
    for j in range(EPG):
        v = row(N_GROUPS + (N_GROUPS - 1) * EPG + j)
        for i in range(N_GROUPS - 2, -1, -1):
            v = jnp.where(is_g[i], row(N_GROUPS + i * EPG + j), v)
        el.append(v)

    def first_max(vals):
        mx = functools.reduce(jnp.maximum, vals)
        seen = jnp.zeros(mx.shape, jnp.bool_)
        hits = []
        for v in vals:
            hit = jnp.logical_and(v == mx, jnp.logical_not(seen))
            hits.append(hit)
            seen = jnp.logical_or(seen, hit)
        return mx, hits

    emax, f1 = first_max(el)
    m2, f2 = first_max([jnp.where(f1[j], NEG_BIG, el[j]) for j in range(EPG)])
    e2 = jnp.exp(m2 - emax)
    w1 = p_group / (1.0 + e2)
    w2 = e2 * w1
    wl = [jnp.where(f1[j], w1, jnp.where(f2[j], w2, 0.0)) for j in range(EPG)]
    zero = jnp.zeros_like(gmax)
    rows = [zero] * N_GROUPS + [jnp.where(is_g[i], wl[j], 0.0) for i in range(N_GROUPS) for j in range(EPG)]
    rows += [zero] * (4 * SUBLANES - len(rows))
    gt = jnp.concatenate([jnp.concatenate(rows, axis=0), jnp.zeros((LANES - 4 * SUBLANES, lt.shape[1]), F32)], axis=0)
    return gt.T


def _outproj_kernel(x_ref, og_ref, oc_ref, od_ref, wg_ref, wc_ref, wd_ref, fg_ref, rw_ref, rb_ref,
                    h_ref, hn_ref, gates_ref):
    h = (x_ref[...] + _dot(og_ref[...], wg_ref[...]) + _dot(oc_ref[...], wc_ref[...])
         + _dot(od_ref[...], wd_ref[...]))
    h_ref[...] = h
    ms = jnp.mean(h * h, axis=-1, keepdims=True)
    hn = h * lax.rsqrt(ms + EPS) * fg_ref[...]
    hn_ref[...] = hn.astype(BF16)
    hi, lo = _split_bf16(hn)
    whi = rw_ref[0]
    wlo = rw_ref[1]
    logits = _dot(hi, whi) + _dot(lo, whi) + _dot(hi, wlo) + rb_ref[...]
    gates_ref[...] = _route(logits)


def _outproj(x2, o_gla, o_conv, o_diff, wg, wc, wd, fg, rw, rb, *, tm):
    T, D = x2.shape
    tile = lambda w: pl.BlockSpec((tm, w), lambda i: (i, 0))
    full = lambda a: pl.BlockSpec(a.shape, lambda i: (0,) * a.ndim)
    consts = (wg, wc, wd, fg, rw, rb)
    return pl.pallas_call(
        _outproj_kernel,
        grid=(T // tm,),
        in_specs=[tile(D), tile(GLA_V), tile(CONV_CH), tile(DIFF_W)] + [full(a) for a in consts],
        out_specs=[tile(D), tile(D), tile(LANES)],
        out_shape=[
            jax.ShapeDtypeStruct((T, D), F32),
            jax.ShapeDtypeStruct((T, D), BF16),
            jax.ShapeDtypeStruct((T, LANES), F32),
        ],
        compiler_params=_params(48, ("parallel",)),
        name="outproj",
    )(x2, o_gla, o_conv, o_diff, *consts)


def _moe_kernel(h_ref, hn_ref, gates_ref, wgu_ref, wd_ref, o_ref, hid_sc):
    t = hn_ref[...]
    gates = gates_ref[...]
    lane = lax.broadcasted_iota(jnp.int32, gates.shape, 1)
    for e in range(N_EXPERTS):
        gu = _dot(t, wgu_ref[e])
        g = gu[:, :D_EXPERT]
        u = gu[:, D_EXPERT:]
        ge = jnp.sum(jnp.where(lane == N_GROUPS + e, gates, 0.0), axis=-1, keepdims=True)
        hid_sc[:, e * D_EXPERT:(e + 1) * D_EXPERT] = (g * _sigmoid(g) * u * ge).astype(BF16)
    o_ref[...] = h_ref[...] + _dot(hid_sc[...], wd_ref[...])


def _moe(h, hn, gates, wgu, wd, *, tm):
    T, D = h.shape
    tile = lambda w: pl.BlockSpec((tm, w), lambda i: (i, 0))
    return pl.pallas_call(
        _moe_kernel,
        grid=(T // tm,),
        in_specs=[tile(D), tile(D), tile(LANES),
                  pl.BlockSpec(wgu.shape, lambda i: (0, 0, 0), pipeline_mode=pl.Buffered(1)),
                  pl.BlockSpec(wd.shape, lambda i: (0, 0), pipeline_mode=pl.Buffered(1))],
        out_specs=tile(D),
        out_shape=jax.ShapeDtypeStruct((T, D), F32),
        scratch_shapes=[pltpu.VMEM((tm, N_EXPERTS * D_EXPERT), BF16)],
        compiler_params=_params(56, ("parallel",)),
        name="moe",
    )(h, hn, gates, wgu, wd)


def _layer(x2, l, p, *, B, S):
    D = x2.shape[1]
    row = lambda a: a.reshape(1, -1).astype(F32)
    w_in = p["w_in"][l]
    o = np.cumsum([0, GLA_QK, GLA_QK, GLA_V, GLA_RANK, GLA_V, 2 * CONV_CH, DIFF_W, DIFF_W, DIFF_W])
    seg = lambda n: w_in[:, o[n]:o[n + 1]]
    wa = jnp.concatenate([seg(0), seg(1), seg(2), seg(4), seg(3),
                          jnp.zeros((D, LANES - GLA_RANK), F32)], axis=1).astype(BF16)
    wc = seg(5).astype(BF16)
    wq = seg(6).astype(BF16)
    wk = seg(7).astype(BF16)
    wvt = seg(8).T.astype(BF16)
    qg = row(jnp.tile(p["diff_qnorm_g"][l], 2 * DIFF_HEADS))
    kg = row(jnp.tile(p["diff_knorm_g"][l], 2 * DIFF_HEADS))
    gla_in, conv_in, qn, kn, vt = _inproj(x2, row(p["mix_norm_g"][l]), wa, wc, wq, wk, wvt, qg, kg, tm=512)

    wgate = jnp.zeros((LANES, GLA_QK), F32).at[:GLA_RANK].set(p["gla_gate_w"][l]).astype(BF16)
    o_gla = _gla(gla_in, wgate, row(p["gla_gate_b"][l]), row(jnp.tile(p["gla_norm_g"][l], GLA_HEADS)), B=B, S=S)

    o_conv = _conv(conv_in, p["conv_w"][l].astype(F32), row(p["conv_b"][l]), row(p["conv_norm_g"][l]),
                   row(p["conv_norm_b"][l]), B=B, S=S, ts=min(512, S))

    lambda_init = 0.8 - 0.6 * math.exp(-0.3 * l)
    o_diff = _attn(qn, kn, vt, row(p["diff_lq1"][l]), row(p["diff_lk1"][l]), row(p["diff_lq2"][l]),
                   row(p["diff_lk2"][l]), p["diff_subln_g"][l].reshape(-1, 1).astype(F32),
                   B=B, S=S, tq=min(512, S), lambda_init=lambda_init)

    w_out = p["w_out"][l].astype(BF16)
    rw = jnp.zeros((D, LANES), F32)
    rw = rw.at[:, :N_GROUPS].set(p["router_group_w"][l]).at[:, N_GROUPS:N_GROUPS + N_EXPERTS].set(p["router_expert_w"][l])
    rw_hi, rw_lo = _split_bf16(rw)
    rb = jnp.zeros((1, LANES), F32)
    rb = rb.at[0, :N_GROUPS].set(p["router_group_b"][l]).at[0, N_GROUPS:N_GROUPS + N_EXPERTS].set(p["router_expert_b"][l])
    h, hn, gates = _outproj(x2, o_gla, o_conv, o_diff, w_out[:GLA_V], w_out[GLA_V:GLA_V + CONV_CH],
                            w_out[GLA_V + CONV_CH:], row(p["ffn_norm_g"][l]), jnp.stack([rw_hi, rw_lo]), rb, tm=512)

    wgu = jnp.concatenate([p["expert_w_gate"][l], p["expert_w_up"][l]], axis=-1).astype(BF16)
    wd = p["expert_w_down"][l].reshape(N_EXPERTS * D_EXPERT, D).astype(BF16)
    return _moe(h, hn, gates, wgu, wd, tm=512)


def kernel(x, mix_norm_g, w_in, gla_gate_w, gla_gate_b, gla_norm_g, conv_w, conv_b, conv_norm_g, conv_norm_b, diff_qnorm_g, diff_knorm_g, diff_lq1, diff_lk1, diff_lq2, diff_lk2, diff_subln_g, w_out, ffn_norm_g, router_group_w, router_group_b, router_expert_w, router_expert_b, expert_w_gate, expert_w_up, expert_w_down):
    p = dict(mix_norm_g=mix_norm_g, w_in=w_in, gla_gate_w=gla_gate_w, gla_gate_b=gla_gate_b,
             gla_norm_g=gla_norm_g, conv_w=conv_w, conv_b=conv_b, conv_norm_g=conv_norm_g,
             conv_norm_b=conv_norm_b, diff_qnorm_g=diff_qnorm_g, diff_knorm_g=diff_knorm_g,
             diff_lq1=diff_lq1, diff_lk1=diff_lk1, diff_lq2=diff_lq2, diff_lk2=diff_lk2,
             diff_subln_g=diff_subln_g, w_out=w_out, ffn_norm_g=ffn_norm_g,
             router_group_w=router_group_w, router_group_b=router_group_b,
             router_expert_w=router_expert_w, router_expert_b=router_expert_b,
             expert_w_gate=expert_w_gate, expert_w_up=expert_w_up, expert_w_down=expert_w_down)
    B, S, D = x.shape
    assert D == 16 * DIFF_D and w_in.shape[-1] == 2 * GLA_QK + 2 * GLA_V + GLA_RANK + 2 * CONV_CH + 3 * DIFF_W
    x2 = x.reshape(B * S, D).astype(F32)
    for l in range(w_in.shape[0]):
        x2 = _layer(x2, l, p, B=B, S=S)
    return x2.reshape(B, S, D)
```

```python
import functools
import math

import numpy as np
import jax
import jax.numpy as jnp
from jax import lax
from jax.experimental import pallas as pl
from jax.experimental.pallas import tpu as pltpu

F32 = jnp.float32
BF16 = jnp.bfloat16
EPS = 1e-6
NEG_BIG = -1e30
LOG2E = math.log2(math.e)

LANES = 128
SUBLANES = 8
MIB = 1024 * 1024

GLA_HEADS = 4
GLA_DK = 32
GLA_DV = 64
GLA_RANK = 16
GLA_TAU = 16.0
GLA_CHUNK = 64
GLA_SUB = 16
CONV_CH = 256
CONV_GROUPS = 4
CONV_WIDTH = 31
CONV_HALO = 32
DIFF_HEADS = 4
DIFF_D = 64
N_GROUPS = 4
EPG = 4
N_EXPERTS = 16
D_EXPERT = 256

GLA_QK = GLA_HEADS * GLA_DK
GLA_V = GLA_HEADS * GLA_DV
GLA_IN = GLA_QK * 2 + GLA_V * 2 + LANES
DIFF_W = DIFF_HEADS * 2 * DIFF_D


def _dot(a, b):
    return jnp.dot(a, b, preferred_element_type=F32)


def _dot_nt(a, b):
    return lax.dot_general(a, b, (((1,), (1,)), ((), ())), preferred_element_type=F32)


def _dot_tn(a, b):
    return lax.dot_general(a, b, (((0,), (0,)), ((), ())), preferred_element_type=F32)


def _split_bf16(x):
    hi = x.astype(BF16)
    lo = (x - hi.astype(F32)).astype(BF16)
    return hi, lo


def _dot_hilo(x, m):
    hi, lo = _split_bf16(x)
    return _dot(hi, m) + _dot(lo, m)


def _sigmoid(x):
    return 1.0 / (1.0 + jnp.exp(-x))


def _group_mean_matrix(width, group):
    idx = np.arange(width) // group
    return jnp.asarray((idx[:, None] == idx[None, :]).astype(np.float32) / group, dtype=BF16)


def _params(vmem_mib, semantics):
    return pltpu.CompilerParams(dimension_semantics=semantics, vmem_limit_bytes=vmem_mib * MIB)


def _residual_in(refs, n_x):
    x = refs[0][...]
    for r in refs[1:n_x]:
        x = x + r[...]
    return x, refs[n_x:]


def _inproj_kernel(*refs, n_x):
    x, (g_ref, wa_ref, wc_ref, wq_ref, wk_ref, wvt_ref, qg_ref, kg_ref, gm_ref,
        gla_ref, conv_ref, q_ref, k_ref, vt_ref) = _residual_in(refs, n_x)
    ms = jnp.mean(x * x, axis=-1, keepdims=True)
    xn = (x * lax.rsqrt(ms + EPS) * g_ref[...]).astype(BF16)
    gla_ref[...] = _dot(xn, wa_ref[...]).astype(BF16)
    conv_ref[...] = _dot(xn, wc_ref[...]).astype(BF16)
    gm = gm_ref[...]

    def head_norm(t, g):
        msq = _dot((t * t).astype(BF16), gm)
        return t * lax.rsqrt(msq + EPS) * g

    q = head_norm(_dot(xn, wq_ref[...]), qg_ref[...])
    q_ref[...] = (q * (DIFF_D ** -0.5 * LOG2E)).astype(BF16)
    k_ref[...] = head_norm(_dot(xn, wk_ref[...]), kg_ref[...]).astype(BF16)
    vt_ref[...] = _dot_nt(wvt_ref[...], xn).astype(BF16)


def _inproj(xs, g, wa, wc, wq, wk, wvt, qg, kg, *, T, tm):
    D = xs[0].shape[1]
    gm = _group_mean_matrix(DIFF_W, DIFF_D)
    full = lambda shape: pl.BlockSpec(shape, lambda i: (0, 0))
    return pl.pallas_call(
        functools.partial(_inproj_kernel, n_x=len(xs)),
        grid=(T // tm,),
        in_specs=[pl.BlockSpec((tm, D), lambda i: (i, 0))] * len(xs) + [
            full((1, D)), full((D, GLA_IN)), full((D, 2 * CONV_CH)), full((D, DIFF_W)), full((D, DIFF_W)),
            full((DIFF_W, D)), full((1, DIFF_W)), full((1, DIFF_W)), full((DIFF_W, DIFF_W)),
        ],
        out_specs=[
            pl.BlockSpec((tm, GLA_IN), lambda i: (i, 0)),
            pl.BlockSpec((tm, 2 * CONV_CH), lambda i: (i, 0)),
            pl.BlockSpec((tm, DIFF_W), lambda i: (i, 0)),
            pl.BlockSpec((tm, DIFF_W), lambda i: (i, 0)),
            pl.BlockSpec((DIFF_W, tm), lambda i: (0, i)),
        ],
        out_shape=[
            jax.ShapeDtypeStruct((T, GLA_IN), BF16),
            jax.ShapeDtypeStruct((T, 2 * CONV_CH), BF16),
            jax.ShapeDtypeStruct((T, DIFF_W), BF16),
            jax.ShapeDtypeStruct((T, DIFF_W), BF16),
            jax.ShapeDtypeStruct((DIFF_W, T), BF16),
        ],
        compiler_params=_params(48, ("parallel",)),
        name="inproj",
    )(*xs, g, wa, wc, wq, wk, wvt, qg, kg, gm)


def _gla_constants(nb):
    C, SUB = GLA_CHUNK, GLA_SUB
    r = np.arange(nb * C)
    tril = ((r[:, None] // C) == (r[None, :] // C)) & (r[None, :] <= r[:, None])
    i = np.arange(C)[:, None]
    col_j = np.tile(np.arange(C), GLA_HEADS)[None, :]
    dist = (i // SUB) - (col_j // SUB)
    sel = np.where(dist == 0, np.where(i >= col_j, 0, -1), np.where(dist > 0, dist, -1))
    kmask = (np.arange(GLA_HEADS * C)[:, None] // C) == (np.arange(GLA_QK)[None, :] // GLA_DK)
    vmask = (np.arange(GLA_HEADS * C)[:, None] // C) == (np.arange(GLA_V)[None, :] // GLA_DV)
    smask = (np.arange(GLA_V)[:, None] // GLA_DV) == (np.arange(GLA_QK)[None, :] // GLA_DK)
    return (jnp.asarray(tril, BF16), jnp.asarray(sel, jnp.int32), jnp.asarray(kmask, F32),
            jnp.asarray(vmask, F32), jnp.asarray(smask, F32))


def _gla_kernel(in_ref, wg_ref, bg_ref, ng_ref, tril_ref, sel_ref, kmask_ref, vmask_ref, smask_ref, gm_ref,
                o_ref, st_ref, *, nb):
    C, SUB = GLA_CHUNK, GLA_SUB
    R = nb * C
    nsub = C // SUB

    @pl.when(pl.program_id(0) == 0)
    def _():
        st_ref[...] = jnp.zeros_like(st_ref)

    sel = sel_ref[...]
    kmask = kmask_ref[...] > 0.5
    vmask = vmask_ref[...] > 0.5
    smask = smask_ref[...]
    scale = GLA_DK ** -0.5
    zero = jnp.zeros((), BF16)

    x = in_ref[...].reshape(R, GLA_IN)
    q = x[:, 0:GLA_QK].astype(F32)
    k = x[:, GLA_QK:2 * GLA_QK].astype(F32)
    v = x[:, 2 * GLA_QK:2 * GLA_QK + GLA_V]
    og = x[:, 2 * GLA_QK + GLA_V:2 * GLA_QK + 2 * GLA_V].astype(F32)
    glr = x[:, 2 * GLA_QK + 2 * GLA_V:]
    z = _dot(glr, wg_ref[...]) + bg_ref[...]
    la = (jnp.minimum(z, 0.0) - jnp.log(1.0 + jnp.exp(-jnp.abs(z)))) * (1.0 / GLA_TAU)
    la_hi, la_lo = _split_bf16(la)
    b2 = _dot(tril_ref[...], jnp.concatenate([la_hi, la_lo], axis=1))
    b = b2[:, :GLA_QK] + b2[:, GLA_QK:]
    ends = b.reshape(R // SUB, SUB, GLA_QK)[:, SUB - 1:SUB, :]
    sub_idx = lax.broadcasted_iota(jnp.int32, ends.shape, 0) & (nsub - 1)

    def back(n):
        shifted = jnp.concatenate([jnp.zeros((n, 1, GLA_QK), F32), ends[:-n]], axis=0)
        return jnp.where(sub_idx >= n, shifted, 0.0)

    expand = lambda t: jnp.broadcast_to(t, (R // SUB, SUB, GLA_QK)).reshape(R, GLA_QK)
    pprev_s, r1_s, r2_s = back(1), back(2), back(3)
    pend = expand(ends)
    pprev = expand(pprev_s)
    qs = q * scale
    qe = qs * jnp.exp(b - pprev)
    qe1 = (qe * expand(jnp.exp(pprev_s - r1_s))).astype(BF16)
    qe2 = (qe * expand(jnp.exp(pprev_s - r2_s))).astype(BF16)
    qe = qe.astype(BF16)
    qeb = (qs * jnp.exp(b)).astype(BF16)
    ke = (k * jnp.exp(pend - b)).astype(BF16)
    kdg = (k * jnp.exp(pprev - b)).astype(BF16)

    outs = []
    for bi in range(nb):
        sl = slice(bi * C, (bi + 1) * C)
        blast = ends[bi * nsub + nsub - 1]
        kbd = jnp.where(kmask, jnp.concatenate([ke[sl]] * GLA_HEADS, axis=0), zero)
        kbd_dg = jnp.where(kmask, jnp.concatenate([kdg[sl]] * GLA_HEADS, axis=0), zero)
        s_off = _dot_nt(jnp.concatenate([qe[sl], qe1[sl], qe2[sl]], axis=0), kbd)
        s_dg = _dot_nt(qe[sl], kbd_dg)
        scores = jnp.where(sel == 0, s_dg,
                           jnp.where(sel == 1, s_off[0:C],
                                     jnp.where(sel == 2, s_off[C:2 * C],
                                               jnp.where(sel == 3, s_off[2 * C:3 * C], 0.0))))
        vbd = jnp.where(vmask, jnp.concatenate([v[sl]] * GLA_HEADS, axis=0), zero)
        st = st_ref[bi]
        outs.append(_dot(scores.astype(BF16), vbd) + _dot_nt(qeb[sl], st.astype(BF16)))
        kd = (k[sl] * jnp.exp(blast - b[sl])).astype(BF16)
        st_ref[bi] = st * jnp.exp(blast) + _dot_tn(v[sl], kd) * smask

    o = jnp.concatenate(outs, axis=0)
    hi, lo = _split_bf16(o * o)
    ms2 = _dot(jnp.concatenate([hi, lo], axis=0), gm_ref[...])
    on = o * lax.rsqrt(ms2[:R] + ms2[R:] + EPS) * ng_ref[...]
    o_ref[...] = (on * (og * _sigmoid(og))).reshape(nb, C, GLA_V).astype(o_ref.dtype)


def _gla(gla_in, wg, bg, ng, *, B, S):
    C = GLA_CHUNK
    x3 = gla_in.reshape(B, S, GLA_IN)
    cm, sel, kmask, vmask, smask = _gla_constants(B)
    gm = _group_mean_matrix(GLA_V, GLA_DV)
    consts = (wg, bg, ng, cm, sel, kmask, vmask, smask, gm)
    full = lambda a: pl.BlockSpec(a.shape, lambda c: (0,) * a.ndim)
    out = pl.pallas_call(
        functools.partial(_gla_kernel, nb=B),
        grid=(S // C,),
        in_specs=[pl.BlockSpec((B, C, GLA_IN), lambda c: (0, c, 0))] + [full(a) for a in consts],
        out_specs=pl.BlockSpec((B, C, GLA_V), lambda c: (0, c, 0)),
        out_shape=jax.ShapeDtypeStruct((B, S, GLA_V), BF16),
        scratch_shapes=[pltpu.VMEM((B, GLA_V, GLA_QK), F32)],
        compiler_params=_params(32, ("arbitrary",)),
        name="gla",
    )(x3, *consts)
    return out.reshape(B * S, GLA_V)


def _conv_kernel(halo_ref, cur_ref, w_ref, b_ref, g_ref, beta_ref, gm_ref, o_ref, z_sc, *, ts, sub):
    def glu(u):
        u = u.astype(F32)
        return u[:, :CONV_CH] * _sigmoid(u[:, CONV_CH:])

    first = pl.program_id(1) == 0
    zh = glu(halo_ref[...])
    z_sc[0, 0:CONV_HALO, :] = jnp.where(first, 0.0, zh)
    z_sc[0, CONV_HALO:, :] = glu(cur_ref[...])
    span = ts + CONV_HALO - SUBLANES
    for s in range(1, SUBLANES):
        z_sc[s, 0:span, :] = z_sc[0, s:s + span, :]
    gm = gm_ref[...]
    off = CONV_HALO - (CONV_WIDTH - 1)
    for r0 in range(0, ts, sub):
        acc = jnp.broadcast_to(b_ref[...], (sub, CONV_CH))
        for w in range(CONV_WIDTH):
            shift = (off + w) % SUBLANES
            base = r0 + off + w - shift
            acc = acc + z_sc[shift, base:base + sub, :] * w_ref[w:w + 1, :]
        mu = _dot_hilo(acc, gm)
        d = acc - mu
        var = _dot_hilo(d * d, gm)
        yn = d * lax.rsqrt(var + EPS) * g_ref[...] + beta_ref[...]
        o_ref[r0:r0 + sub, :] = (yn * _sigmoid(yn)).astype(o_ref.dtype)


def _conv(conv_in, w, b, g, beta, *, B, S, ts):
    T = B * S
    nt = S // ts
    hb = ts // CONV_HALO
    gm = _group_mean_matrix(CONV_CH, CONV_CH // CONV_GROUPS)
    full = lambda a: pl.BlockSpec(a.shape, lambda bb, i: (0,) * a.ndim)
    consts = (w, b, g, beta, gm)
    return pl.pallas_call(
        functools.partial(_conv_kernel, ts=ts, sub=64),
        grid=(B, nt),
        in_specs=[
            pl.BlockSpec((CONV_HALO, 2 * CONV_CH), lambda bb, i: (jnp.maximum((bb * nt + i) * hb - 1, 0), 0)),
            pl.BlockSpec((ts, 2 * CONV_CH), lambda bb, i: (bb * nt + i, 0)),
        ] + [full(a) for a in consts],
        out_specs=pl.BlockSpec((ts, CONV_CH), lambda bb, i: (bb * nt + i, 0)),
        out_shape=jax.ShapeDtypeStruct((T, CONV_CH), BF16),
        scratch_shapes=[pltpu.VMEM((SUBLANES, ts + CONV_HALO, CONV_CH), F32)],
        compiler_params=_params(32, ("parallel", "parallel")),
        name="conv",
    )(conv_in, conv_in, *consts)


def _attn_kernel(q_ref, k_ref, vt_ref, lq1_ref, lk1_ref, lq2_ref, lk2_ref, sg_ref, o_ref, acc_sc, sa_sc, sb_sc,
                 *, tq, lambda_init):
    tk = tq
    i = pl.program_id(2)
    q = q_ref[...]
    lane = lax.broadcasted_iota(jnp.int32, (tk, 2 * DIFF_D), 1)
    first_map = lane < DIFF_D
    zero = jnp.zeros((), BF16)
    acc_sc[...] = jnp.zeros_like(acc_sc)

    def scores_into(j, dst):
        start = pl.multiple_of(j * tk, tk)
        kb = k_ref[pl.ds(start, tk), :]
        kk = jnp.concatenate([jnp.where(first_map, kb, zero), jnp.where(first_map, zero, kb)], axis=0)
        dst[...] = _dot_nt(kk, q)

    def softmax_pv(j, src, carry, masked):
        start = pl.multiple_of(j * tk, tk)
        vt = vt_ref[:, pl.ds(start, tk)]
        if masked:
            key = lax.broadcasted_iota(jnp.int32, (tk, tq), 0)
            qry = lax.broadcasted_iota(jnp.int32, (tk, tq), 1)
            keep = key <= qry
        out = []
        for m in range(2):
            mp, lp = carry[2 * m], carry[2 * m + 1]
            sm = src[m * tk:(m + 1) * tk, :]
            if masked:
                sm = jnp.where(keep, sm, NEG_BIG)
            mn = jnp.maximum(mp, jnp.max(sm, axis=0, keepdims=True))
            alpha = jnp.exp2(mp - mn)
            p = jnp.exp2(sm - mn)
            ln = alpha * lp + jnp.sum(p, axis=0, keepdims=True)
            acc_sc[m] = alpha * acc_sc[m] + _dot(vt, p.astype(BF16))
            out += [mn, ln]
        return tuple(out)

    def pair(jj, carry):
        j = 2 * jj
        scores_into(j + 1, sb_sc)
        carry = softmax_pv(j, sa_sc, carry, False)
        scores_into(j + 2, sa_sc)
        return softmax_pv(j + 1, sb_sc, carry, False)

    def tail_odd(carry):
        scores_into(i, sb_sc)
        carry = softmax_pv(i - 1, sa_sc, carry, False)
        return softmax_pv(i, sb_sc, carry, True)

    def tail_even(carry):
        return softmax_pv(i, sa_sc, carry, True)

    scores_into(0, sa_sc)
    init = (jnp.full((1, tq), NEG_BIG, F32), jnp.zeros((1, tq), F32)) * 2
    carry = lax.fori_loop(0, lax.shift_right_logical(i, 1), pair, init)
    m1, l1, m2, l2 = lax.cond(lax.rem(i, 2) == 1, tail_odd, tail_even, carry)

    lam = (jnp.exp(jnp.sum(lq1_ref[...] * lk1_ref[...], keepdims=True))
           - jnp.exp(jnp.sum(lq2_ref[...] * lk2_ref[...], keepdims=True)) + lambda_init)
    a = acc_sc[0] / l1 - lam * (acc_sc[1] / l2)
    ms = jnp.mean(a * a, axis=0, keepdims=True)
    y = a * lax.rsqrt(ms + EPS) * sg_ref[...] * (1.0 - lambda_init)
    o_ref[...] = y.T.astype(o_ref.dtype)


def _attn(qn, kn, vt, lq1, lk1, lq2, lk2, sg, *, B, S, tq, lambda_init):
    T = B * S
    nq = S // tq
    W = 2 * DIFF_D
    small = lambda a: pl.BlockSpec(a.shape, lambda b, h, i: (0,) * a.ndim)
    consts = (lq1, lk1, lq2, lk2, sg)
    return pl.pallas_call(
        functools.partial(_attn_kernel, tq=tq, lambda_init=lambda_init),
        grid=(B, DIFF_HEADS, nq),
        in_specs=[
            pl.BlockSpec((tq, W), lambda b, h, i: (b * nq + i, h)),
            pl.BlockSpec((S, W), lambda b, h, i: (b, h)),
            pl.BlockSpec((W, S), lambda b, h, i: (h, b)),
        ] + [small(a) for a in consts],
        out_specs=pl.BlockSpec((tq, W), lambda b, h, i: (b * nq + i, h)),
        out_shape=jax.ShapeDtypeStruct((T, DIFF_W), BF16),
        scratch_shapes=[pltpu.VMEM((2, W, tq), F32), pltpu.VMEM((2 * tq, tq), F32), pltpu.VMEM((2 * tq, tq), F32)],
        compiler_params=_params(48, ("parallel", "parallel", "arbitrary")),
        name="diffattn",
    )(qn, kn, vt, *consts)


def _route(logits):
    lt = logits.T
    row = lambda r: lt[r:r + 1, :]
    g = [row(i) for i in range(N_GROUPS)]
    gmax = functools.reduce(jnp.maximum, g)
    p_group = 1.0 / sum(jnp.exp(gi - gmax) for gi in g)
    taken = jnp.zeros(gmax.shape, jnp.bool_)
    is_g = []
    for gi in g:
        hit = jnp.logical_and(gi == gmax, jnp.logical_not(taken))
        is_g.append(hit)
        taken = jnp.logical_or(taken, hit)
    el = []
    for j in range(EPG):
        v = row(N_GROUPS + (N_GROUPS - 1) * EPG + j)
        for i in range(N_GROUPS - 2, -1, -1):
            v = jnp.where(is_g[i], row(N_GROUPS + i * EPG + j), v)
        el.append(v)

    def first_max(vals):
        mx = functools.reduce(jnp.maximum, vals)
        seen = jnp.zeros(mx.shape, jnp.bool_)
        hits = []
        for v in vals:
            hit = jnp.logical_and(v == mx, jnp.logical_not(seen))
            hits.append(hit)
            seen = jnp.logical_or(seen, hit)
        return mx, hits

    emax, f1 = first_max(el)
    m2, f2 = first_max([jnp.where(f1[j], NEG_BIG, el[j]) for j in range(EPG)])
    e2 = jnp.exp(m2 - emax)
    w1 = p_group / (1.0 + e2)
    w2 = e2 * w1
    wl = [jnp.where(f1[j], w1, jnp.where(f2[j], w2, 0.0)) for j in range(EPG)]
    zero = jnp.zeros_like(gmax)
    gidx = functools.reduce(lambda acc, i: jnp.where(is_g[i], float(i), acc), range(1, N_GROUPS), zero)
    rows = [gidx] + [zero] * (N_GROUPS - 1) + [jnp.where(is_g[i], wl[j], 0.0) for i in range(N_GROUPS) for j in range(EPG)]
    rows += [zero] * (4 * SUBLANES - len(rows))
    gt = jnp.concatenate([jnp.concatenate(rows, axis=0), jnp.zeros((LANES - 4 * SUBLANES, lt.shape[1]), F32)], axis=0)
    return gt.T


def _outproj_kernel(*refs, n_x):
    x, (og_ref, oc_ref, od_ref, wg_ref, wc_ref, wd_ref, fg_ref, rw_ref, rb_ref, h_ref, hx_ref) = _residual_in(refs, n_x)
    D = x.shape[1]
    h = x + _dot(og_ref[...], wg_ref[...]) + _dot(oc_ref[...], wc_ref[...]) + _dot(od_ref[...], wd_ref[...])
    h_ref[...] = h
    ms = jnp.mean(h * h, axis=-1, keepdims=True)
    hn = h * lax.rsqrt(ms + EPS) * fg_ref[...]
    hi, lo = _split_bf16(hn)
    whi = rw_ref[0]
    wlo = rw_ref[1]
    logits = _dot(hi, whi) + _dot(lo, whi) + _dot(hi, wlo) + rb_ref[...]
    hx_ref[:, :D] = hn
    hx_ref[:, D:] = _route(logits)


def _outproj(xs, o_gla, o_conv, o_diff, wg, wc, wd, fg, rw, rb, *, T, tm):
    D = xs[0].shape[1]
    tile = lambda w: pl.BlockSpec((tm, w), lambda i: (i, 0))
    full = lambda a: pl.BlockSpec(a.shape, lambda i: (0,) * a.ndim)
    consts = (wg, wc, wd, fg, rw, rb)
    return pl.pallas_call(
        functools.partial(_outproj_kernel, n_x=len(xs)),
        grid=(T // tm,),
        in_specs=[tile(D)] * len(xs) + [tile(GLA_V), tile(CONV_CH), tile(DIFF_W)] + [full(a) for a in consts],
        out_specs=[tile(D), tile(D + LANES)],
        out_shape=[
            jax.ShapeDtypeStruct((T, D), F32),
            jax.ShapeDtypeStruct((T, D + LANES), F32),
        ],
        compiler_params=_params(48, ("parallel",)),
        name="outproj",
    )(*xs, o_gla, o_conv, o_diff, *consts)


def _moe_plan(gid, *, T, tm, nt):
    i32 = jnp.int32
    order = jnp.argsort(gid).astype(i32)
    counts = jnp.sum((gid[:, None] == jnp.arange(N_GROUPS, dtype=i32)[None, :]).astype(i32), axis=0)
    starts = jnp.cumsum(counts) - counts
    tiles = (counts + tm - 1) // tm
    tile_end = jnp.cumsum(tiles)
    n = jnp.arange(nt, dtype=i32)
    tg = jnp.minimum(jnp.sum((n[:, None] >= tile_end[None, :]).astype(i32), axis=1), N_GROUPS - 1)
    r = jnp.arange(tm, dtype=i32)[None, :]
    rank = (n - (tile_end - tiles)[tg])[:, None] * tm + r
    valid = (n[:, None] < tile_end[-1]) & (rank < counts[tg][:, None])
    src = jnp.where(valid, order[jnp.clip(starts[tg][:, None] + rank, 0, T - 1)], 0)
    dst = jnp.where(valid, src, T + (n[:, None] % 2) * tm + r)
    return tg, src.reshape(nt, 1, tm), dst.reshape(nt, 1, tm)


def _moe_kernel(tg_ref, src_ref, nxt_ref, dst_ref, hx_hbm, wgu_a, wd_a, wgu_b, wd_b, y_hbm,
                xbuf, ybuf, hid_sc, gsem, ssem, *, tm, D):
    m = pl.program_id(0)
    last = pl.num_programs(0) - 1

    def gather(idx_ref, p, s):
        for r in range(tm):
            pltpu.make_async_copy(hx_hbm.at[pl.ds(idx_ref[p, 0, r], 1), :], xbuf.at[s, pl.ds(r, 1), :],
                                  gsem.at[s]).start()

    def gather_wait(s):
        pltpu.make_async_copy(hx_hbm.at[pl.ds(0, tm), :], xbuf.at[s], gsem.at[s]).wait()

    def scatter(s):
        for r in range(tm):
            pltpu.make_async_copy(ybuf.at[s, pl.ds(r, 1), :], y_hbm.at[pl.ds(dst_ref[s, 0, r], 1), :],
                                  ssem.at[s]).start()

    def scatter_wait(s):
        pltpu.make_async_copy(ybuf.at[s], y_hbm.at[pl.ds(0, tm), :], ssem.at[s]).wait()

    def experts(s, wgu_ref, wd_ref):
        xb = xbuf[s]
        t = xb[:, :D].astype(BF16)
        gates = xb[:, D:]
        lane = lax.broadcasted_iota(jnp.int32, gates.shape, 1)
        first = N_GROUPS + tg_ref[2 * m + s] * EPG
        for j in range(EPG):
            gu = _dot(t, wgu_ref[0, j])
            g = gu[:, :D_EXPERT]
            u = gu[:, D_EXPERT:]
            ge = jnp.sum(jnp.where(lane == first + j, gates, 0.0), axis=-1, keepdims=True)
            hid_sc[s, :, j * D_EXPERT:(j + 1) * D_EXPERT] = (g * _sigmoid(g) * u * ge).astype(BF16)
        ybuf[s] = _dot(hid_sc[s], wd_ref[0])

    @pl.when(m == 0)
    def _():
        gather(src_ref, 0, 0)

    @pl.when(m > 0)
    def _():
        scatter_wait(0)
        scatter_wait(1)

    gather_wait(0)
    gather(src_ref, 1, 1)
    experts(0, wgu_a, wd_a)
    scatter(0)
    gather_wait(1)
    gather(nxt_ref, 0, 0)
    experts(1, wgu_b, wd_b)
    scatter(1)

    @pl.when(m == last)
    def _():
        gather_wait(0)
        scatter_wait(0)
        scatter_wait(1)


def _moe(hx, gid, wgu, wd, *, T, tm):
    D = hx.shape[1] - LANES
    nt = T // tm + N_GROUPS
    nt += nt % 2
    tg, src, dst = _moe_plan(gid, T=T, tm=tm, nt=nt)
    smem = lambda rows, f: pl.BlockSpec((rows, 1, tm), f, memory_space=pltpu.SMEM)
    wgu_spec = lambda s: pl.BlockSpec((1, EPG, D, 2 * D_EXPERT), lambda m, tg: (tg[2 * m + s], 0, 0, 0))
    wd_spec = lambda s: pl.BlockSpec((1, EPG * D_EXPERT, D), lambda m, tg: (tg[2 * m + s], 0, 0))
    grid_spec = pltpu.PrefetchScalarGridSpec(
        num_scalar_prefetch=1,
        grid=(nt // 2,),
        in_specs=[
            smem(2, lambda m, tg: (m, 0, 0)),
            smem(1, lambda m, tg: (jnp.minimum(2 * m + 2, nt - 1), 0, 0)),
            smem(2, lambda m, tg: (m, 0, 0)),
            pl.BlockSpec(memory_space=pl.ANY),
            wgu_spec(0), wd_spec(0), wgu_spec(1), wd_spec(1),
        ],
        out_specs=pl.BlockSpec(memory_space=pl.ANY),
        scratch_shapes=[
            pltpu.VMEM((2, tm, D + LANES), F32), pltpu.VMEM((2, tm, D), F32),
            pltpu.VMEM((2, tm, EPG * D_EXPERT), BF16),
            pltpu.SemaphoreType.DMA((2,)), pltpu.SemaphoreType.DMA((2,)),
        ],
    )
    return pl.pallas_call(
        functools.partial(_moe_kernel, tm=tm, D=D),
        grid_spec=grid_spec,
        out_shape=jax.ShapeDtypeStruct((T + 2 * tm, D), F32),
        compiler_params=_params(48, ("arbitrary",)),
        name="moe",
    )(tg, src, src, dst, hx, wgu, wd, wgu, wd)


def _add_kernel(a_ref, b_ref, o_ref):
    o_ref[...] = a_ref[...] + b_ref[...]


def _add(a, b, *, T, tm):
    D = a.shape[1]
    tile = pl.BlockSpec((tm, D), lambda i: (i, 0))
    return pl.pallas_call(
        _add_kernel, grid=(T // tm,), in_specs=[tile, tile], out_specs=tile,
        out_shape=jax.ShapeDtypeStruct((T, D), F32),
        compiler_params=_params(32, ("parallel",)), name="residual_add",
    )(a, b)


def _layer(xs, l, p, *, B, S):
    D = xs[0].shape[1]
    T = B * S
    row = lambda a: a.reshape(1, -1).astype(F32)
    w_in = p["w_in"][l]
    o = np.cumsum([0, GLA_QK, GLA_QK, GLA_V, GLA_RANK, GLA_V, 2 * CONV_CH, DIFF_W, DIFF_W, DIFF_W])
    seg = lambda n: w_in[:, o[n]:o[n + 1]]
    wa = jnp.concatenate([seg(0), seg(1), seg(2), seg(4), seg(3),
                          jnp.zeros((D, LANES - GLA_RANK), F32)], axis=1).astype(BF16)
    wc = seg(5).astype(BF16)
    wq = seg(6).astype(BF16)
    wk = seg(7).astype(BF16)
    wvt = seg(8).T.astype(BF16)
    qg = row(jnp.tile(p["diff_qnorm_g"][l], 2 * DIFF_HEADS))
    kg = row(jnp.tile(p["diff_knorm_g"][l], 2 * DIFF_HEADS))
    gla_in, conv_in, qn, kn, vt = _inproj(xs, row(p["mix_norm_g"][l]), wa, wc, wq, wk, wvt, qg, kg, T=T, tm=512)

    wgate = jnp.zeros((LANES, GLA_QK), F32).at[:GLA_RANK].set(p["gla_gate_w"][l]).astype(BF16)
    o_gla = _gla(gla_in, wgate, row(p["gla_gate_b"][l]), row(jnp.tile(p["gla_norm_g"][l], GLA_HEADS)), B=B, S=S)

    o_conv = _conv(conv_in, p["conv_w"][l].astype(F32), row(p["conv_b"][l]), row(p["conv_norm_g"][l]),
                   row(p["conv_norm_b"][l]), B=B, S=S, ts=min(512, S))

    lambda_init = 0.8 - 0.6 * math.exp(-0.3 * l)
    o_diff = _attn(qn, kn, vt, row(p["diff_lq1"][l]), row(p["diff_lk1"][l]), row(p["diff_lq2"][l]),
                   row(p["diff_lk2"][l]), p["diff_subln_g"][l].reshape(-1, 1).astype(F32),
                   B=B, S=S, tq=min(512, S), lambda_init=lambda_init)

    w_out = p["w_out"][l].astype(BF16)
    rw = jnp.zeros((D, LANES), F32)
    rw = rw.at[:, :N_GROUPS].set(p["router_group_w"][l]).at[:, N_GROUPS:N_GROUPS + N_EXPERTS].set(p["router_expert_w"][l])
    rw_hi, rw_lo = _split_bf16(rw)
    rb = jnp.zeros((1, LANES), F32)
    rb = rb.at[0, :N_GROUPS].set(p["router_group_b"][l]).at[0, N_GROUPS:N_GROUPS + N_EXPERTS].set(p["router_expert_b"][l])
    h, hx = _outproj(xs, o_gla, o_conv, o_diff, w_out[:GLA_V], w_out[GLA_V:GLA_V + CONV_CH],
                     w_out[GLA_V + CONV_CH:], row(p["ffn_norm_g"][l]), jnp.stack([rw_hi, rw_lo]), rb, T=T, tm=512)

    wgu = jnp.concatenate([p["expert_w_gate"][l], p["expert_w_up"][l]], axis=-1).astype(BF16)
    wgu = wgu.reshape(N_GROUPS, EPG, D, 2 * D_EXPERT)
    wd = p["expert_w_down"][l].reshape(N_GROUPS, EPG * D_EXPERT, D).astype(BF16)
    gid = hx[:T, D].astype(jnp.int32)
    return h, _moe(hx, gid, wgu, wd, T=T, tm=512)


def kernel(x, mix_norm_g, w_in, gla_gate_w, gla_gate_b, gla_norm_g, conv_w, conv_b, conv_norm_g, conv_norm_b, diff_qnorm_g, diff_knorm_g, diff_lq1, diff_lk1, diff_lq2, diff_lk2, diff_subln_g, w_out, ffn_norm_g, router_group_w, router_group_b, router_expert_w, router_expert_b, expert_w_gate, expert_w_up, expert_w_down):
    p = dict(mix_norm_g=mix_norm_g, w_in=w_in, gla_gate_w=gla_gate_w, gla_gate_b=gla_gate_b,
             gla_norm_g=gla_norm_g, conv_w=conv_w, conv_b=conv_b, conv_norm_g=conv_norm_g,
             conv_norm_b=conv_norm_b, diff_qnorm_g=diff_qnorm_g, diff_knorm_g=diff_knorm_g,
             diff_lq1=diff_lq1, diff_lk1=diff_lk1, diff_lq2=diff_lq2, diff_lk2=diff_lk2,
             diff_subln_g=diff_subln_g, w_out=w_out, ffn_norm_g=ffn_norm_g,
             router_group_w=router_group_w, router_group_b=router_group_b,
             router_expert_w=router_expert_w, router_expert_b=router_expert_b,
             expert_w_gate=expert_w_gate, expert_w_up=expert_w_up, expert_w_down=expert_w_down)
    B, S, D = x.shape
    assert D == 16 * DIFF_D and w_in.shape[-1] == 2 * GLA_QK + 2 * GLA_V + GLA_RANK + 2 * CONV_CH + 3 * DIFF_W
    xs = (x.reshape(B * S, D).astype(F32),)
    for l in range(w_in.shape[0]):
        xs = _layer(xs, l, p, B=B, S=S)
    return _add(*xs, T=B * S, tm=512).reshape(B, S, D)
```

```python
import functools
import math

import numpy as np
import jax
import jax.numpy as jnp
from jax import lax
from jax.experimental import pallas as pl
from jax.experimental.pallas import tpu as pltpu

F32 = jnp.float32
BF16 = jnp.bfloat16
EPS = 1e-6
NEG_BIG = -1e30
LOG2E = math.log2(math.e)

LANES = 128
SUBLANES = 8
MIB = 1024 * 1024

GLA_HEADS = 4
GLA_DK = 32
GLA_DV = 64
GLA_RANK = 16
GLA_TAU = 16.0
GLA_CHUNK = 64
GLA_SUB = 16
CONV_CH = 256
CONV_GROUPS = 4
CONV_WIDTH = 31
CONV_HALO = 32
DIFF_HEADS = 4
DIFF_D = 64
N_GROUPS = 4
EPG = 4
N_EXPERTS = 16
D_EXPERT = 256

GLA_QK = GLA_HEADS * GLA_DK
GLA_V = GLA_HEADS * GLA_DV
GLA_IN = GLA_QK * 2 + GLA_V * 2 + LANES
DIFF_W = DIFF_HEADS * 2 * DIFF_D


def _dot(a, b):
    return jnp.dot(a, b, preferred_element_type=F32)


def _dot_nt(a, b):
    return lax.dot_general(a, b, (((1,), (1,)), ((), ())), preferred_element_type=F32)


def _dot_tn(a, b):
    return lax.dot_general(a, b, (((0,), (0,)), ((), ())), preferred_element_type=F32)


def _split_bf16(x):
    hi = x.astype(BF16)
    lo = (x - hi.astype(F32)).astype(BF16)
    return hi, lo


def _dot_hilo(x, m):
    hi, lo = _split_bf16(x)
    return _dot(hi, m) + _dot(lo, m)


def _sigmoid(x):
    return 1.0 / (1.0 + jnp.exp(-x))


def _group_mean_matrix(width, group):
    idx = np.arange(width) // group
    return jnp.asarray((idx[:, None] == idx[None, :]).astype(np.float32) / group, dtype=BF16)


def _params(vmem_mib, semantics):
    return pltpu.CompilerParams(dimension_semantics=semantics, vmem_limit_bytes=vmem_mib * MIB)


def _inproj_kernel(x_ref, g_ref, wa_ref, wc_ref, wq_ref, wk_ref, wvt_ref, qg_ref, kg_ref, gm_ref,
                   gla_ref, conv_ref, q_ref, k_ref, vt_ref):
    x = x_ref[...]
    ms = jnp.mean(x * x, axis=-1, keepdims=True)
    xn = (x * lax.rsqrt(ms + EPS) * g_ref[...]).astype(BF16)
    gla_ref[...] = _dot(xn, wa_ref[...]).astype(BF16)
    conv_ref[...] = _dot(xn, wc_ref[...]).astype(BF16)
    gm = gm_ref[...]

    def head_norm(t, g):
        msq = _dot((t * t).astype(BF16), gm)
        return t * lax.rsqrt(msq + EPS) * g

    q = head_norm(_dot(xn, wq_ref[...]), qg_ref[...])
    q_ref[...] = (q * (DIFF_D ** -0.5 * LOG2E)).astype(BF16)
    k_ref[...] = head_norm(_dot(xn, wk_ref[...]), kg_ref[...]).astype(BF16)
    vt_ref[...] = _dot_nt(wvt_ref[...], xn).astype(BF16)


def _inproj(x2, g, wa, wc, wq, wk, wvt, qg, kg, *, T, tm):
    D = x2.shape[1]
    gm = _group_mean_matrix(DIFF_W, DIFF_D)
    full = lambda shape: pl.BlockSpec(shape, lambda i: (0, 0))
    return pl.pallas_call(
        _inproj_kernel,
        grid=(T // tm,),
        in_specs=[
            pl.BlockSpec((tm, D), lambda i: (i, 0)),
            full((1, D)), full((D, GLA_IN)), full((D, 2 * CONV_CH)), full((D, DIFF_W)), full((D, DIFF_W)),
            full((DIFF_W, D)), full((1, DIFF_W)), full((1, DIFF_W)), full((DIFF_W, DIFF_W)),
        ],
        out_specs=[
            pl.BlockSpec((tm, GLA_IN), lambda i: (i, 0)),
            pl.BlockSpec((tm, 2 * CONV_CH), lambda i: (i, 0)),
            pl.BlockSpec((tm, DIFF_W), lambda i: (i, 0)),
            pl.BlockSpec((tm, DIFF_W), lambda i: (i, 0)),
            pl.BlockSpec((DIFF_W, tm), lambda i: (0, i)),
        ],
        out_shape=[
            jax.ShapeDtypeStruct((T, GLA_IN), BF16),
            jax.ShapeDtypeStruct((T, 2 * CONV_CH), BF16),
            jax.ShapeDtypeStruct((T, DIFF_W), BF16),
            jax.ShapeDtypeStruct((T, DIFF_W), BF16),
            jax.ShapeDtypeStruct((DIFF_W, T), BF16),
        ],
        compiler_params=_params(48, ("parallel",)),
        name="inproj",
    )(x2, g, wa, wc, wq, wk, wvt, qg, kg, gm)


def _gla_constants(nb):
    C, SUB = GLA_CHUNK, GLA_SUB
    r = np.arange(nb * C)
    tril = ((r[:, None] // C) == (r[None, :] // C)) & (r[None, :] <= r[:, None])
    i = np.arange(C)[:, None]
    col_j = np.tile(np.arange(C), GLA_HEADS)[None, :]
    dist = (i // SUB) - (col_j // SUB)
    sel = np.where(dist == 0, np.where(i >= col_j, 0, -1), np.where(dist > 0, dist, -1))
    kmask = (np.arange(GLA_HEADS * C)[:, None] // C) == (np.arange(GLA_QK)[None, :] // GLA_DK)
    vmask = (np.arange(GLA_HEADS * C)[:, None] // C) == (np.arange(GLA_V)[None, :] // GLA_DV)
    smask = (np.arange(GLA_V)[:, None] // GLA_DV) == (np.arange(GLA_QK)[None, :] // GLA_DK)
    return (jnp.asarray(tril, BF16), jnp.asarray(sel, jnp.int32), jnp.asarray(kmask, F32),
            jnp.asarray(vmask, F32), jnp.asarray(smask, F32))


def _gla_kernel(in_ref, wg_ref, bg_ref, ng_ref, tril_ref, sel_ref, kmask_ref, vmask_ref, smask_ref, gm_ref,
                o_ref, st_ref, *, nb):
    C, SUB = GLA_CHUNK, GLA_SUB
    R = nb * C
    nsub = C // SUB

    @pl.when(pl.program_id(0) == 0)
    def _():
        st_ref[...] = jnp.zeros_like(st_ref)

    sel = sel_ref[...]
    kmask = kmask_ref[...] > 0.5
    vmask = vmask_ref[...] > 0.5
    smask = smask_ref[...]
    scale = GLA_DK ** -0.5
    zero = jnp.zeros((), BF16)

    x = in_ref[...].reshape(R, GLA_IN)
    q = x[:, 0:GLA_QK].astype(F32)
    k = x[:, GLA_QK:2 * GLA_QK].astype(F32)
    v = x[:, 2 * GLA_QK:2 * GLA_QK + GLA_V]
    og = x[:, 2 * GLA_QK + GLA_V:2 * GLA_QK + 2 * GLA_V].astype(F32)
    glr = x[:, 2 * GLA_QK + 2 * GLA_V:]
    z = _dot(glr, wg_ref[...]) + bg_ref[...]
    la = (jnp.minimum(z, 0.0) - jnp.log(1.0 + jnp.exp(-jnp.abs(z)))) * (1.0 / GLA_TAU)
    la_hi, la_lo = _split_bf16(la)
    b2 = _dot(tril_ref[...], jnp.concatenate([la_hi, la_lo], axis=1))
    b = b2[:, :GLA_QK] + b2[:, GLA_QK:]
    ends = b.reshape(R // SUB, SUB, GLA_QK)[:, SUB - 1:SUB, :]
    sub_idx = lax.broadcasted_iota(jnp.int32, ends.shape, 0) & (nsub - 1)

    def back(n):
        shifted = jnp.concatenate([jnp.zeros((n, 1, GLA_QK), F32), ends[:-n]], axis=0)
        return jnp.where(sub_idx >= n, shifted, 0.0)

    expand = lambda t: jnp.broadcast_to(t, (R // SUB, SUB, GLA_QK)).reshape(R, GLA_QK)
    pprev_s, r1_s, r2_s = back(1), back(2), back(3)
    pend = expand(ends)
    pprev = expand(pprev_s)
    qs = q * scale
    qe = qs * jnp.exp(b - pprev)
    qe1 = (qe * expand(jnp.exp(pprev_s - r1_s))).astype(BF16)
    qe2 = (qe * expand(jnp.exp(pprev_s - r2_s))).astype(BF16)
    qe = qe.astype(BF16)
    qeb = (qs * jnp.exp(b)).astype(BF16)
    ke = (k * jnp.exp(pend - b)).astype(BF16)
    kdg = (k * jnp.exp(pprev - b)).astype(BF16)

    outs = []
    for bi in range(nb):
        sl = slice(bi * C, (bi + 1) * C)
        blast = ends[bi * nsub + nsub - 1]
        kbd = jnp.where(kmask, jnp.concatenate([ke[sl]] * GLA_HEADS, axis=0), zero)
        kbd_dg = jnp.where(kmask, jnp.concatenate([kdg[sl]] * GLA_HEADS, axis=0), zero)
        s_off = _dot_nt(jnp.concatenate([qe[sl], qe1[sl], qe2[sl]], axis=0), kbd)
        s_dg = _dot_nt(qe[sl], kbd_dg)
        scores = jnp.where(sel == 0, s_dg,
                           jnp.where(sel == 1, s_off[0:C],
                                     jnp.where(sel == 2, s_off[C:2 * C],
                                               jnp.where(sel == 3, s_off[2 * C:3 * C], 0.0))))
        vbd = jnp.where(vmask, jnp.concatenate([v[sl]] * GLA_HEADS, axis=0), zero)
        st = st_ref[bi]
        outs.append(_dot(scores.astype(BF16), vbd) + _dot_nt(qeb[sl], st.astype(BF16)))
        kd = (k[sl] * jnp.exp(blast - b[sl])).astype(BF16)
        st_ref[bi] = st * jnp.exp(blast) + _dot_tn(v[sl], kd) * smask

    o = jnp.concatenate(outs, axis=0)
    hi, lo = _split_bf16(o * o)
    ms2 = _dot(jnp.concatenate([hi, lo], axis=0), gm_ref[...])
    on = o * lax.rsqrt(ms2[:R] + ms2[R:] + EPS) * ng_ref[...]
    o_ref[...] = (on * (og * _sigmoid(og))).reshape(nb, C, GLA_V).astype(o_ref.dtype)


def _gla(gla_in, wg, bg, ng, *, B, S):
    C = GLA_CHUNK
    x3 = gla_in.reshape(B, S, GLA_IN)
    cm, sel, kmask, vmask, smask = _gla_constants(B)
    gm = _group_mean_matrix(GLA_V, GLA_DV)
    consts = (wg, bg, ng, cm, sel, kmask, vmask, smask, gm)
    full = lambda a: pl.BlockSpec(a.shape, lambda c: (0,) * a.ndim)
    out = pl.pallas_call(
        functools.partial(_gla_kernel, nb=B),
        grid=(S // C,),
        in_specs=[pl.BlockSpec((B, C, GLA_IN), lambda c: (0, c, 0))] + [full(a) for a in consts],
        out_specs=pl.BlockSpec((B, C, GLA_V), lambda c: (0, c, 0)),
        out_shape=jax.ShapeDtypeStruct((B, S, GLA_V), BF16),
        scratch_shapes=[pltpu.VMEM((B, GLA_V, GLA_QK), F32)],
        compiler_params=_params(32, ("arbitrary",)),
        name="gla",
    )(x3, *consts)
    return out.reshape(B * S, GLA_V)


def _conv_kernel(halo_ref, cur_ref, w_ref, b_ref, g_ref, beta_ref, gm_ref, o_ref, z_sc, *, ts, sub):
    def glu(u):
        u = u.astype(F32)
        return u[:, :CONV_CH] * _sigmoid(u[:, CONV_CH:])

    first = pl.program_id(1) == 0
    zh = glu(halo_ref[...])
    z_sc[0, 0:CONV_HALO, :] = jnp.where(first, 0.0, zh)
    z_sc[0, CONV_HALO:, :] = glu(cur_ref[...])
    span = ts + CONV_HALO - SUBLANES
    for s in range(1, SUBLANES):
        z_sc[s, 0:span, :] = z_sc[0, s:s + span, :]
    gm = gm_ref[...]
    off = CONV_HALO - (CONV_WIDTH - 1)
    for r0 in range(0, ts, sub):
        acc = jnp.broadcast_to(b_ref[...], (sub, CONV_CH))
        for w in range(CONV_WIDTH):
            shift = (off + w) % SUBLANES
            base = r0 + off + w - shift
            acc = acc + z_sc[shift, base:base + sub, :] * w_ref[w:w + 1, :]
        mu = _dot_hilo(acc, gm)
        d = acc - mu
        var = _dot_hilo(d * d, gm)
        yn = d * lax.rsqrt(var + EPS) * g_ref[...] + beta_ref[...]
        o_ref[r0:r0 + sub, :] = (yn * _sigmoid(yn)).astype(o_ref.dtype)


def _conv(conv_in, w, b, g, beta, *, B, S, ts):
    T = B * S
    nt = S // ts
    hb = ts // CONV_HALO
    gm = _group_mean_matrix(CONV_CH, CONV_CH // CONV_GROUPS)
    full = lambda a: pl.BlockSpec(a.shape, lambda bb, i: (0,) * a.ndim)
    consts = (w, b, g, beta, gm)
    return pl.pallas_call(
        functools.partial(_conv_kernel, ts=ts, sub=64),
        grid=(B, nt),
        in_specs=[
            pl.BlockSpec((CONV_HALO, 2 * CONV_CH), lambda bb, i: (jnp.maximum((bb * nt + i) * hb - 1, 0), 0)),
            pl.BlockSpec((ts, 2 * CONV_CH), lambda bb, i: (bb * nt + i, 0)),
        ] + [full(a) for a in consts],
        out_specs=pl.BlockSpec((ts, CONV_CH), lambda bb, i: (bb * nt + i, 0)),
        out_shape=jax.ShapeDtypeStruct((T, CONV_CH), BF16),
        scratch_shapes=[pltpu.VMEM((SUBLANES, ts + CONV_HALO, CONV_CH), F32)],
        compiler_params=_params(32, ("parallel", "parallel")),
        name="conv",
    )(conv_in, conv_in, *consts)


def _attn_kernel(qa_ref, qb_ref, k_ref, vt_ref, lq1_ref, lk1_ref, lq2_ref, lk2_ref, sg_ref, o_ref,
                 acc_sc, s_sc, *, tq, nq, lambda_init):
    tk = tq
    n_slots = s_sc.shape[0]
    lane = lax.broadcasted_iota(jnp.int32, (tk, 2 * DIFF_D), 1)
    first_map = lane < DIFF_D
    zero = jnp.zeros((), BF16)
    ones = jnp.ones((2 * SUBLANES, tk), BF16)
    causal = lax.broadcasted_iota(jnp.int32, (tk, tq), 0) <= lax.broadcasted_iota(jnp.int32, (tk, tq), 1)
    lam = (jnp.exp(jnp.sum(lq1_ref[...] * lk1_ref[...], keepdims=True))
           - jnp.exp(jnp.sum(lq2_ref[...] * lk2_ref[...], keepdims=True)) + lambda_init)

    def unit(q, blk, j, kk, state, masked, slot):
        s_sc[slot] = _dot_nt(kk, q)
        vt = jnp.concatenate([vt_ref[:, j * tk:(j + 1) * tk], ones], axis=0)
        out = []
        for m in range(2):
            sm = s_sc[slot, m * tk:(m + 1) * tk, :]
            if masked:
                sm = jnp.where(causal, sm, NEG_BIG)
            mn = jnp.max(sm, axis=0, keepdims=True)
            if state is not None:
                mn = jnp.maximum(state[2 * m], mn)
            pv = _dot(vt, jnp.exp2(sm - mn).astype(BF16))
            ln = pv[2 * DIFF_D:2 * DIFF_D + 1]
            pv = pv[:2 * DIFF_D]
            if state is None:
                acc_sc[blk, m] = pv
            else:
                alpha = jnp.exp2(state[2 * m] - mn)
                ln = alpha * state[2 * m + 1] + ln
                acc_sc[blk, m] = alpha * acc_sc[blk, m] + pv
            out += [mn, ln]
        return tuple(out)

    def finalize(blk, state):
        _, l1, _, l2 = state
        a = acc_sc[blk, 0] / l1 - lam * (acc_sc[blk, 1] / l2)
        ms = jnp.mean(a * a, axis=0, keepdims=True)
        y = a * lax.rsqrt(ms + EPS) * sg_ref[...] * (1.0 - lambda_init)
        o_ref[blk * tq:(blk + 1) * tq, :] = y.T.astype(o_ref.dtype)

    def branch(pa):
        pb = nq - 1 - pa

        def run():
            qa = qa_ref[...]
            qb = qb_ref[...]
            sa = sb = None
            n = 0
            for j in range(pb + 1):
                kb = k_ref[j * tk:(j + 1) * tk, :]
                kk = jnp.concatenate([jnp.where(first_map, kb, zero), jnp.where(first_map, zero, kb)], axis=0)
                if j <= pa:
                    sa = unit(qa, 0, j, kk, sa, j == pa, n % n_slots)
                    n += 1
                    if j == pa:
                        finalize(0, sa)
                sb = unit(qb, 1, j, kk, sb, j == pb, n % n_slots)
                n += 1
            finalize(1, sb)
        return run

    lax.switch(pl.program_id(2), [branch(pa) for pa in range(nq // 2)])


def _attn_row_block(i, nq):
    b, qb = i // nq, i % nq
    low = qb < nq // 2
    return (b * (nq // 2) + jnp.where(low, qb, nq - 1 - qb)) * 2 + jnp.where(low, 0, 1)


def _attn(qn, kn, vt, lq1, lk1, lq2, lk2, sg, *, B, S, tq, lambda_init):
    T = B * S
    nq = S // tq
    assert nq % 2 == 0
    W = 2 * DIFF_D
    small = lambda a: pl.BlockSpec(a.shape, lambda b, h, p: (0,) * a.ndim)
    blk_a = pl.BlockSpec((tq, W), lambda b, h, p: (b * nq + p, h))
    blk_b = pl.BlockSpec((tq, W), lambda b, h, p: (b * nq + nq - 1 - p, h))
    consts = (lq1, lk1, lq2, lk2, sg)
    return pl.pallas_call(
        functools.partial(_attn_kernel, tq=tq, nq=nq, lambda_init=lambda_init),
        grid=(B, DIFF_HEADS, nq // 2),
        in_specs=[
            blk_a, blk_b,
            pl.BlockSpec((S, W), lambda b, h, p: (b, h)),
            pl.BlockSpec((W, S), lambda b, h, p: (h, b)),
        ] + [small(a) for a in consts],
        out_specs=pl.BlockSpec((2 * tq, W), lambda b, h, p: (b * (nq // 2) + p, h)),
        out_shape=jax.ShapeDtypeStruct((T, DIFF_W), BF16),
        scratch_shapes=[pltpu.VMEM((2, 2, W, tq), F32), pltpu.VMEM((3, 2 * tq, tq), F32)],
        compiler_params=_params(48, ("parallel", "parallel", "arbitrary")),
        name="diffattn",
    )(qn, qn, kn, vt, *consts)


def _route(logits):
    lt = logits.T
    row = lambda r: lt[r:r + 1, :]
    g = [row(i) for i in range(N_GROUPS)]
    gmax = functools.reduce(jnp.maximum, g)
    p_group = 1.0 / sum(jnp.exp(gi - gmax) for gi in g)
    taken = jnp.zeros(gmax.shape, jnp.bool_)
    is_g = []
    for gi in g:
        hit = jnp.logical_and(gi == gmax, jnp.logical_not(taken))
        is_g.append(hit)
        taken = jnp.logical_or(taken, hit)
    el = []
    for j in range(EPG):
        v = row(N_GROUPS + (N_GROUPS - 1) * EPG + j)
        for i in range(N_GROUPS - 2, -1, -1):
            v = jnp.where(is_g[i], row(N_GROUPS + i * EPG + j), v)
        el.append(v)

    def first_max(vals):
        mx = functools.reduce(jnp.maximum, vals)
        seen = jnp.zeros(mx.shape, jnp.bool_)
        hits = []
        for v in vals:
            hit = jnp.logical_and(v == mx, jnp.logical_not(seen))
            hits.append(hit)
            seen = jnp.logical_or(seen, hit)
        return mx, hits

    emax, f1 = first_max(el)
    m2, f2 = first_max([jnp.where(f1[j], NEG_BIG, el[j]) for j in range(EPG)])
    e2 = jnp.exp(m2 - emax)
    w1 = p_group / (1.0 + e2)
    w2 = e2 * w1
    wl = [jnp.where(f1[j], w1, jnp.where(f2[j], w2, 0.0)) for j in range(EPG)]
    zero = jnp.zeros_like(gmax)
    gidx = functools.reduce(lambda acc, i: jnp.where(is_g[i], float(i), acc), range(1, N_GROUPS), zero)
    rows = [gidx] + [zero] * (N_GROUPS - 1) + [jnp.where(is_g[i], wl[j], 0.0) for i in range(N_GROUPS) for j in range(EPG)]
    rows += [zero] * (4 * SUBLANES - len(rows))
    gt = jnp.concatenate([jnp.concatenate(rows, axis=0), jnp.zeros((LANES - 4 * SUBLANES, lt.shape[1]), F32)], axis=0)
    return gt.T


def _outproj_kernel(x_ref, og_ref, oc_ref, od_ref, wg_ref, wc_ref, wd_ref, fg_ref, rw_ref, rb_ref, hx_ref):
    D = x_ref.shape[1]
    h = (x_ref[...] + _dot(og_ref[...], wg_ref[...]) + _dot(oc_ref[...], wc_ref[...])
         + _dot(od_ref[...], wd_ref[...]))
    ms = jnp.mean(h * h, axis=-1, keepdims=True)
    hn = h * lax.rsqrt(ms + EPS) * fg_ref[...]
    hi, lo = _split_bf16(hn)
    whi = rw_ref[0]
    wlo = rw_ref[1]
    logits = _dot(hi, whi) + _dot(lo, whi) + _dot(hi, wlo) + rb_ref[...]
    hx_ref[:, :D] = h
    hx_ref[:, D:] = _route(logits)


def _outproj(x2, o_gla, o_conv, o_diff, wg, wc, wd, fg, rw, rb, *, T, tm, attn_nq):
    D = x2.shape[1]
    tile = lambda w: pl.BlockSpec((tm, w), lambda i: (i, 0))
    full = lambda a: pl.BlockSpec(a.shape, lambda i: (0,) * a.ndim)
    consts = (wg, wc, wd, fg, rw, rb)
    return pl.pallas_call(
        _outproj_kernel,
        grid=(T // tm,),
        in_specs=[tile(D), tile(GLA_V), tile(CONV_CH),
                  pl.BlockSpec((tm, DIFF_W), lambda i: (_attn_row_block(i, attn_nq), 0))] + [full(a) for a in consts],
        out_specs=tile(D + LANES),
        out_shape=jax.ShapeDtypeStruct((T, D + LANES), F32),
        compiler_params=_params(48, ("parallel",)),
        name="outproj",
    )(x2, o_gla, o_conv, o_diff, *consts)


def _moe_plan(gid, *, T, tm, nt):
    i32 = jnp.int32
    order = jnp.argsort(gid).astype(i32)
    counts = jnp.sum((gid[:, None] == jnp.arange(N_GROUPS, dtype=i32)[None, :]).astype(i32), axis=0)
    starts = jnp.cumsum(counts) - counts
    tiles = (counts + tm - 1) // tm
    tile_end = jnp.cumsum(tiles)
    n = jnp.arange(nt, dtype=i32)
    tg = jnp.minimum(jnp.sum((n[:, None] >= tile_end[None, :]).astype(i32), axis=1), N_GROUPS - 1)
    r = jnp.arange(tm, dtype=i32)[None, :]
    rank = (n - (tile_end - tiles)[tg])[:, None] * tm + r
    valid = (n[:, None] < tile_end[-1]) & (rank < counts[tg][:, None])
    src = jnp.where(valid, order[jnp.clip(starts[tg][:, None] + rank, 0, T - 1)], 0)
    dst = jnp.where(valid, src, T + (n[:, None] % 2) * tm + r)
    return tg, src.reshape(nt, 1, tm), dst.reshape(nt, 1, tm)


def _moe_kernel(tg_ref, src_ref, nxt_ref, dst_ref, hx_hbm, fg_ref, wgu_a, wd_a, wgu_b, wd_b, y_hbm,
                xbuf, ybuf, hid_sc, gsem, ssem, *, tm, D):
    m = pl.program_id(0)
    last = pl.num_programs(0) - 1

    def gather(idx_ref, p, s):
        for r in range(tm):
            pltpu.make_async_copy(hx_hbm.at[pl.ds(idx_ref[p, 0, r], 1), :], xbuf.at[s, pl.ds(r, 1), :],
                                  gsem.at[s]).start(priority=r % 2)

    def gather_wait(s):
        pltpu.make_async_copy(hx_hbm.at[pl.ds(0, tm), :], xbuf.at[s], gsem.at[s]).wait()

    def scatter(s):
        for r in range(tm):
            pltpu.make_async_copy(ybuf.at[s, pl.ds(r, 1), :], y_hbm.at[pl.ds(dst_ref[s, 0, r], 1), :],
                                  ssem.at[s]).start(priority=r % 2)

    def scatter_wait(s):
        pltpu.make_async_copy(ybuf.at[s], y_hbm.at[pl.ds(0, tm), :], ssem.at[s]).wait()

    def experts(s, wgu_ref, wd_ref):
        xb = xbuf[s]
        h = xb[:, :D]
        ms = jnp.mean(h * h, axis=-1, keepdims=True)
        t = (h * lax.rsqrt(ms + EPS) * fg_ref[...]).astype(BF16)
        gates = xb[:, D:]
        lane = lax.broadcasted_iota(jnp.int32, gates.shape, 1)
        first = N_GROUPS + tg_ref[2 * m + s] * EPG
        for j in range(EPG):
            gu = _dot(t, wgu_ref[0, j])
            g = gu[:, :D_EXPERT]
            u = gu[:, D_EXPERT:]
            ge = jnp.sum(jnp.where(lane == first + j, gates, 0.0), axis=-1, keepdims=True)
            hid_sc[s, :, j * D_EXPERT:(j + 1) * D_EXPERT] = (g * _sigmoid(g) * u * ge).astype(BF16)
        ybuf[s] = h + _dot(hid_sc[s], wd_ref[0])

    @pl.when(m == 0)
    def _():
        gather(src_ref, 0, 0)

    @pl.when(m > 0)
    def _():
        scatter_wait(0)
        scatter_wait(1)

    gather_wait(0)
    gather(src_ref, 1, 1)
    experts(0, wgu_a, wd_a)
    scatter(0)
    gather_wait(1)
    gather(nxt_ref, 0, 0)
    experts(1, wgu_b, wd_b)
    scatter(1)

    @pl.when(m == last)
    def _():
        gather_wait(0)
        scatter_wait(0)
        scatter_wait(1)


def _moe(hx, gid, fg, wgu, wd, *, T, tm):
    D = hx.shape[1] - LANES
    nt = T // tm + N_GROUPS
    nt += nt % 2
    tg, src, dst = _moe_plan(gid, T=T, tm=tm, nt=nt)
    smem = lambda rows, f: pl.BlockSpec((rows, 1, tm), f, memory_space=pltpu.SMEM)
    wgu_spec = lambda s: pl.BlockSpec((1, EPG, D, 2 * D_EXPERT), lambda m, tg: (tg[2 * m + s], 0, 0, 0))
    wd_spec = lambda s: pl.BlockSpec((1, EPG * D_EXPERT, D), lambda m, tg: (tg[2 * m + s], 0, 0))
    grid_spec = pltpu.PrefetchScalarGridSpec(
        num_scalar_prefetch=1,
        grid=(nt // 2,),
        in_specs=[
            smem(2, lambda m, tg: (m, 0, 0)),
            smem(1, lambda m, tg: (jnp.minimum(2 * m + 2, nt - 1), 0, 0)),
            smem(2, lambda m, tg: (m, 0, 0)),
            pl.BlockSpec(memory_space=pl.ANY),
            pl.BlockSpec(fg.shape, lambda m, tg: (0, 0)),
            wgu_spec(0), wd_spec(0), wgu_spec(1), wd_spec(1),
        ],
        out_specs=pl.BlockSpec(memory_space=pl.ANY),
        scratch_shapes=[
            pltpu.VMEM((2, tm, D + LANES), F32), pltpu.VMEM((2, tm, D), F32),
            pltpu.VMEM((2, tm, EPG * D_EXPERT), BF16),
            pltpu.SemaphoreType.DMA((2,)), pltpu.SemaphoreType.DMA((2,)),
        ],
    )
    return pl.pallas_call(
        functools.partial(_moe_kernel, tm=tm, D=D),
        grid_spec=grid_spec,
        out_shape=jax.ShapeDtypeStruct((T + 2 * tm, D), F32),
        compiler_params=_params(48, ("arbitrary",)),
        name="moe",
    )(tg, src, src, dst, hx, fg, wgu, wd, wgu, wd)


def _layer(x2, l, p, *, B, S):
    D = x2.shape[1]
    T = B * S
    row = lambda a: a.reshape(1, -1).astype(F32)
    w_in = p["w_in"][l]
    o = np.cumsum([0, GLA_QK, GLA_QK, GLA_V, GLA_RANK, GLA_V, 2 * CONV_CH, DIFF_W, DIFF_W, DIFF_W])
    seg = lambda n: w_in[:, o[n]:o[n + 1]]
    wa = jnp.concatenate([seg(0), seg(1), seg(2), seg(4), seg(3),
                          jnp.zeros((D, LANES - GLA_RANK), F32)], axis=1).astype(BF16)
    wc = seg(5).astype(BF16)
    wq = seg(6).astype(BF16)
    wk = seg(7).astype(BF16)
    wvt = seg(8).T.astype(BF16)
    qg = row(jnp.tile(p["diff_qnorm_g"][l], 2 * DIFF_HEADS))
    kg = row(jnp.tile(p["diff_knorm_g"][l], 2 * DIFF_HEADS))
    gla_in, conv_in, qn, kn, vt = _inproj(x2, row(p["mix_norm_g"][l]), wa, wc, wq, wk, wvt, qg, kg, T=T, tm=512)

    wgate = jnp.zeros((LANES, GLA_QK), F32).at[:GLA_RANK].set(p["gla_gate_w"][l]).astype(BF16)
    o_gla = _gla(gla_in, wgate, row(p["gla_gate_b"][l]), row(jnp.tile(p["gla_norm_g"][l], GLA_HEADS)), B=B, S=S)

    o_conv = _conv(conv_in, p["conv_w"][l].astype(F32), row(p["conv_b"][l]), row(p["conv_norm_g"][l]),
                   row(p["conv_norm_b"][l]), B=B, S=S, ts=min(512, S))

    lambda_init = 0.8 - 0.6 * math.exp(-0.3 * l)
    tq = 512
    assert S % (2 * tq) == 0
    o_diff = _attn(qn, kn, vt, row(p["diff_lq1"][l]), row(p["diff_lk1"][l]), row(p["diff_lq2"][l]),
                   row(p["diff_lk2"][l]), p["diff_subln_g"][l].reshape(-1, 1).astype(F32),
                   B=B, S=S, tq=tq, lambda_init=lambda_init)

    w_out = p["w_out"][l].astype(BF16)
    rw = jnp.zeros((D, LANES), F32)
    rw = rw.at[:, :N_GROUPS].set(p["router_group_w"][l]).at[:, N_GROUPS:N_GROUPS + N_EXPERTS].set(p["router_expert_w"][l])
    rw_hi, rw_lo = _split_bf16(rw)
    rb = jnp.zeros((1, LANES), F32)
    rb = rb.at[0, :N_GROUPS].set(p["router_group_b"][l]).at[0, N_GROUPS:N_GROUPS + N_EXPERTS].set(p["router_expert_b"][l])
    fg = row(p["ffn_norm_g"][l])
    hx = _outproj(x2, o_gla, o_conv, o_diff, w_out[:GLA_V], w_out[GLA_V:GLA_V + CONV_CH],
                  w_out[GLA_V + CONV_CH:], fg, jnp.stack([rw_hi, rw_lo]), rb, T=T, tm=tq, attn_nq=S // tq)

    wgu = jnp.concatenate([p["expert_w_gate"][l], p["expert_w_up"][l]], axis=-1).astype(BF16)
    wgu = wgu.reshape(N_GROUPS, EPG, D, 2 * D_EXPERT)
    wd = p["expert_w_down"][l].reshape(N_GROUPS, EPG * D_EXPERT, D).astype(BF16)
    gid = hx[:T, D].astype(jnp.int32)
    return _moe(hx, gid, fg, wgu, wd, T=T, tm=512)


def kernel(x, mix_norm_g, w_in, gla_gate_w, gla_gate_b, gla_norm_g, conv_w, conv_b, conv_norm_g, conv_norm_b, diff_qnorm_g, diff_knorm_g, diff_lq1, diff_lk1, diff_lq2, diff_lk2, diff_subln_g, w_out, ffn_norm_g, router_group_w, router_group_b, router_expert_w, router_expert_b, expert_w_gate, expert_w_up, expert_w_down):
    p = dict(mix_norm_g=mix_norm_g, w_in=w_in, gla_gate_w=gla_gate_w, gla_gate_b=gla_gate_b,
             gla_norm_g=gla_norm_g, conv_w=conv_w, conv_b=conv_b, conv_norm_g=conv_norm_g,
             conv_norm_b=conv_norm_b, diff_qnorm_g=diff_qnorm_g, diff_knorm_g=diff_knorm_g,
             diff_lq1=diff_lq1, diff_lk1=diff_lk1, diff_lq2=diff_lq2, diff_lk2=diff_lk2,
             diff_subln_g=diff_subln_g, w_out=w_out, ffn_norm_g=ffn_norm_g,
             router_group_w=router_group_w, router_group_b=router_group_b,
             router_expert_w=router_expert_w, router_expert_b=router_expert_b,
             expert_w_gate=expert_w_gate, expert_w_up=expert_w_up, expert_w_down=expert_w_down)
    B, S, D = x.shape
    assert D == 16 * DIFF_D and w_in.shape[-1] == 2 * GLA_QK + 2 * GLA_V + GLA_RANK + 2 * CONV_CH + 3 * DIFF_W
    x2 = x.reshape(B * S, D).astype(F32)
    for l in range(w_in.shape[0]):
        x2 = _layer(x2, l, p, B=B, S=S)
    return x2[:B * S].reshape(B, S, D)
```
